```python
import math
import jax
import jax.numpy as jnp
from jax import lax
import numpy as np

D_MODEL = 2048
BATCH = 2
SEQ = 4096
DEPTH = 4
DEC_BATCH = 128
DEC_SEQ = 4
PAST_LEN = 8192
PAGE_SIZE = 128

N_MIXERS = 3
N_MLA_LAYERS = (DEPTH + 2) // 3
N_HGRN_LAYERS = (DEPTH + 1) // 3
N_LRU_LAYERS = DEPTH // 3

NORM_EPS = 1e-6
N_MOD = 9
FFN_RES = 0.5
D_FF = ((8 * D_MODEL // 3 + 255) // 256) * 256

MLA_HEADS = D_MODEL // 128
Q_LORA = D_MODEL // 4
KV_LORA = D_MODEL // 4
QK_NOPE = 128
QK_ROPE = 64
V_HEAD = 128
MLA_QK = QK_NOPE + QK_ROPE
KV_ROW = KV_LORA + QK_ROPE
MLA_SCALE = MLA_QK ** -0.5
ROPE_THETA = 10000.0
Q_BLOCK = 128
NEG_INF = -1e30

HG_EXPAND = 128
HG_HEADS = D_MODEL // HG_EXPAND
HG_DK = HG_EXPAND
HG_DV = D_MODEL // HG_HEADS
HG_CHUNK = 16

LRU_WIDTH = D_MODEL
LRU_BLOCKS = 16
LRU_BW = LRU_WIDTH // LRU_BLOCKS
CONV_W = 4
LRU_C = 8.0

kernel_name = 'hybrid_mla_hgrn2_rglru_macaron_step'


def rmsnorm(x, g):
    xf = x.astype(jnp.float32)
    y = xf * lax.rsqrt(jnp.mean(xf * xf, axis=-1, keepdims=True) + NORM_EPS)
    return (y * g.astype(jnp.float32)).astype(x.dtype)


def modulate(h, shift, scale):
    return h * (1 + scale[:, None, :]) + shift[:, None, :]


def swiglu(h, wg, wu, wd):
    return (jax.nn.silu(h @ wg) * (h @ wu)) @ wd


def apply_rope(x, pos):
    half = x.shape[-1] // 2
    inv = ROPE_THETA ** (-jnp.arange(half, dtype=jnp.float32) / half)
    ang = pos.astype(jnp.float32)[:, None] * inv[None, :]
    cos = jnp.cos(ang)[None, :, None, :]
    sin = jnp.sin(ang)[None, :, None, :]
    xf = x.astype(jnp.float32)
    x1, x2 = xf[..., :half], xf[..., half:]
    return jnp.concatenate([x1 * cos - x2 * sin, x2 * cos + x1 * sin], -1).astype(x.dtype)


def mla_project(h, pos, w_dq, g_q, w_uq, w_dkv, g_kv):
    B, L, _ = h.shape
    q = (rmsnorm(h @ w_dq, g_q) @ w_uq).reshape(B, L, MLA_HEADS, MLA_QK)
    q_nope = q[..., :QK_NOPE]
    q_rope = apply_rope(q[..., QK_NOPE:], pos)
    ckr = h @ w_dkv
    c_kv = rmsnorm(ckr[..., :KV_LORA], g_kv)
    k_rope = apply_rope(ckr[..., None, KV_LORA:], pos)[:, :, 0, :]
    return q_nope, q_rope, jnp.concatenate([c_kv, k_rope], -1)


def mla_attend_prompt(q_nope, q_rope, kv_row, w_uk, w_uv):
    B, L = q_nope.shape[:2]
    c_kv, k_rope = kv_row[..., :KV_LORA], kv_row[..., KV_LORA:]
    k_nope = jnp.einsum('blc,chd->blhd', c_kv, w_uk)
    v = jnp.einsum('blc,chd->blhd', c_kv, w_uv)
    nb = L // Q_BLOCK
    qn = q_nope.reshape(B, nb, Q_BLOCK, MLA_HEADS, QK_NOPE).transpose(1, 0, 2, 3, 4)
    qr = q_rope.reshape(B, nb, Q_BLOCK, MLA_HEADS, QK_ROPE).transpose(1, 0, 2, 3, 4)
    kpos = jnp.arange(L)

    def block(args):
        qn_b, qr_b, start = args
        s = (jnp.einsum('bqhd,bkhd->bhqk', qn_b, k_nope)
             + jnp.einsum('bqhr,bkr->bhqk', qr_b, k_rope)).astype(jnp.float32) * MLA_SCALE
        qpos = start + jnp.arange(Q_BLOCK)
        s = jnp.where(kpos[None, :] <= qpos[:, None], s, NEG_INF)
        p = jax.nn.softmax(s, axis=-1).astype(v.dtype)
        return jnp.einsum('bhqk,bkhd->bqhd', p, v)

    o = lax.map(block, (qn, qr, jnp.arange(nb) * Q_BLOCK))
    return o.transpose(1, 0, 2, 3, 4).reshape(B, L, MLA_HEADS * V_HEAD)


def mla_attend_sample(q_nope, q_rope, kv_new, kv_cache, layer, page_table, w_uk, w_uv):
    Bd, T = q_nope.shape[:2]
    q_cat = jnp.concatenate([jnp.einsum('bthd,chd->bthc', q_nope, w_uk), q_rope], -1)
    kpos = jnp.arange(PAST_LEN + T)
    qpos = PAST_LEN + jnp.arange(T)
    mask = kpos[None, :] <= qpos[:, None]

    def one_seq(args):
        qc, new, pages = args
        past = kv_cache[layer, pages].reshape(-1, KV_ROW).astype(new.dtype)
        keys = jnp.concatenate([past, new], axis=0)
        s = jnp.einsum('thc,sc->hts', qc, keys).astype(jnp.float32) * MLA_SCALE
        s = jnp.where(mask[None], s, NEG_INF)
        p = jax.nn.softmax(s, axis=-1).astype(keys.dtype)
        return jnp.einsum('hts,sc->thc', p, keys[:, :KV_LORA])

    o_lat = lax.map(one_seq, (q_cat, kv_new, page_table))
    return jnp.einsum('bthc,chd->bthd', o_lat, w_uv).reshape(Bd, T, MLA_HEADS * V_HEAD)


def hgrn_lower_bounds(lb_param):
    cs = jnp.cumsum(jax.nn.softmax(lb_param.astype(jnp.float32), axis=0), axis=0)
    return cs - cs[0]


def hgrn2_chunked(q, k, v, logf, S0):
    B, L, H, DK = q.shape
    DV = v.shape[-1]
    C = math.gcd(L, HG_CHUNK)
    n = L // C

    def to_chunks(a):
        return a.reshape(B, n, C, H, a.shape[-1]).transpose(1, 0, 3, 2, 4)

    qc, kc, vc, gc = to_chunks(q), to_chunks(k), to_chunks(v), to_chunks(logf)
    causal = jnp.tril(jnp.ones((C, C), dtype=bool))

    def step(S, inp):
        qb, kb, vb, gb = inp
        b = jnp.cumsum(gb.astype(jnp.float32), axis=2)
        kf = kb.astype(jnp.float32)
        vf = vb.astype(jnp.float32)
        qg = qb.astype(jnp.float32) * jnp.exp(b)
        A = jnp.einsum('bhtk,bhsk->bhts', qg, kf * jnp.exp(-b))
        A = jnp.where(causal, A, 0.0)
        o = jnp.einsum('bhtk,bhkv->bhtv', qg, S) + jnp.einsum('bhts,bhsv->bhtv', A, vf)
        b_last = b[:, :, -1:, :]
        S_new = (jnp.exp(b_last[:, :, 0, :])[..., None] * S
                 + jnp.einsum('bhsk,bhsv->bhkv', kf * jnp.exp(b_last - b), vf))
        return S_new, o

    S, o = lax.scan(step, S0.astype(jnp.float32), (qc, kc, vc, gc))
    o = o.transpose(1, 0, 3, 2, 4).reshape(B, L, H, DV)
    return o.astype(v.dtype), S


def hgrn2_mixer(h, S0, lb, w_q, w_f, w_i, w_g, g_norm, w_o):
    B, L, _ = h.shape
    q = jax.nn.silu(h @ w_q).reshape(B, L, HG_HEADS, HG_DK)
    fz = (h @ w_f).astype(jnp.float32)
    logf = jnp.logaddexp(jnp.log(lb), jnp.log1p(-lb) + jax.nn.log_sigmoid(fz))
    k = (1 - lb) * jax.nn.sigmoid(-fz)
    logf = logf.reshape(B, L, HG_HEADS, HG_DK)
    k = k.reshape(B, L, HG_HEADS, HG_DK)
    v = (h @ w_i).reshape(B, L, HG_HEADS, HG_DV)
    o, S = hgrn2_chunked(q, k, v, logf, S0)
    o = rmsnorm(o, g_norm) * jax.nn.silu(h @ w_g).reshape(B, L, HG_HEADS, HG_DV)
    return o.reshape(B, L, HG_HEADS * HG_DV) @ w_o, S


def causal_conv(u, buf, w, b):
    L = u.shape[1]
    ext = jnp.concatenate([buf.astype(u.dtype), u], axis=1)
    y = ext[:, 0:L] * w[0]
    for j in range(1, CONV_W):
        y = y + ext[:, j:j + L] * w[j]
    return y + b, ext[:, -(CONV_W - 1):]


def linear_scan(a, b, h0):
    def comb(left, right):
        al, bl = left
        ar, br = right
        return al * ar, ar * bl + br
    A, Bc = lax.associative_scan(comb, (a, b), axis=1)
    return Bc + A * h0[:, None, :]


def rglru_mixer(h, h0, conv_buf, w_x, w_y, conv_w, conv_b, w_ra, b_ra, w_ix, b_ix, lam, w_o):
    B, L, _ = h.shape
    gate = jax.nn.gelu(h @ w_y, approximate=True)
    xc, new_buf = causal_conv(h @ w_x, conv_buf, conv_w, conv_b)
    xb = xc.reshape(B, L, LRU_BLOCKS, LRU_BW)
    r = jax.nn.sigmoid(jnp.einsum('blnd,nde->blne', xb, w_ra).reshape(B, L, LRU_WIDTH) + b_ra)
    i = jax.nn.sigmoid(jnp.einsum('blnd,nde->blne', xb, w_ix).reshape(B, L, LRU_WIDTH) + b_ix)
    log_a = -LRU_C * r.astype(jnp.float32) * jax.nn.softplus(-lam.astype(jnp.float32))
    a = jnp.exp(log_a)
    mult = jnp.sqrt(-jnp.expm1(2.0 * log_a))
    hs = linear_scan(a, mult * (i * xc).astype(jnp.float32), h0.astype(jnp.float32))
    y = (hs.astype(h.dtype) * gate) @ w_o
    return y, hs[:, -1], new_buf


def run_trunk(x, c, pos, kv_cache, page_table, hg_state, lru_h, lru_conv, p):
    B, L, D = x.shape
    prompt = kv_cache is None
    cmod = jax.nn.silu(c)
    lbs = hgrn_lower_bounds(p['hg_lb'])
    kv_rows, hg_out, lh_out, lc_out = [], [], [], []
    for li in range(DEPTH):
        mod = (cmod @ p['ada_w'][li] + p['ada_b'][li]).reshape(B, N_MOD, D)
        h = modulate(rmsnorm(x, p['norm_pre'][li, 0]), mod[:, 0], mod[:, 1])
        y = swiglu(h, p['ffn1_wg'][li], p['ffn1_wu'][li], p['ffn1_wd'][li])
        x = x + FFN_RES * mod[:, 2, None, :] * rmsnorm(y, p['norm_post'][li, 0])
        h = modulate(rmsnorm(x, p['norm_pre'][li, 1]), mod[:, 3], mod[:, 4])
        kind, j = li % N_MIXERS, li // N_MIXERS
        if kind == 0:
            qn, qr, kvr = mla_project(h, pos, p['mla_w_dq'][j], p['mla_g_q'][j], p['mla_w_uq'][j],
                                      p['mla_w_dkv'][j], p['mla_g_kv'][j])
            if prompt:
                o = mla_attend_prompt(qn, qr, kvr, p['mla_w_uk'][j], p['mla_w_uv'][j])
            else:
                o = mla_attend_sample(qn, qr, kvr, kv_cache, j, page_table, p['mla_w_uk'][j], p['mla_w_uv'][j])
            y = o @ p['mla_w_o'][j]
            kv_rows.append(kvr)
        elif kind == 1:
            s0 = jnp.zeros((B, HG_HEADS, HG_DK, HG_DV), x.dtype) if prompt else hg_state[j]
            y, s = hgrn2_mixer(h, s0, lbs[li], p['hg_w_q'][j], p['hg_w_f'][j], p['hg_w_i'][j],
                               p['hg_w_g'][j], p['hg_g_norm'][j], p['hg_w_o'][j])
            hg_out.append(s.astype(s0.dtype))
        else:
            h0 = jnp.zeros((B, LRU_WIDTH), x.dtype) if prompt else lru_h[j]
            cb0 = jnp.zeros((B, CONV_W - 1, LRU_WIDTH), x.dtype) if prompt else lru_conv[j]
            y, hT, cb = rglru_mixer(h, h0, cb0, p['lru_w_x'][j], p['lru_w_y'][j], p['lru_conv_w'][j],
                                    p['lru_conv_b'][j], p['lru_w_ra'][j], p['lru_b_ra'][j],
                                    p['lru_w_ix'][j], p['lru_b_ix'][j], p['lru_lam'][j], p['lru_w_o'][j])
            lh_out.append(hT.astype(h0.dtype))
            lc_out.append(cb.astype(cb0.dtype))
        x = x + mod[:, 5, None, :] * rmsnorm(y, p['norm_post'][li, 1])
        h = modulate(rmsnorm(x, p['norm_pre'][li, 2]), mod[:, 6], mod[:, 7])
        y = swiglu(h, p['ffn2_wg'][li], p['ffn2_wu'][li], p['ffn2_wd'][li])
        x = x + FFN_RES * mod[:, 8, None, :] * rmsnorm(y, p['norm_post'][li, 2])
    return x, jnp.stack(kv_rows), jnp.stack(hg_out), jnp.stack(lh_out), jnp.stack(lc_out)


def setup_inputs(seed: int = 0) -> dict:
    key = jax.random.key(seed)
    ks = iter(jax.random.split(key, 64))
    f32 = jnp.float32
    D = D_MODEL

    def nrm(shape, scale=1.0):
        return jax.random.normal(next(ks), shape, f32) * scale

    def gain(shape):
        return 1.0 + nrm(shape, 0.02)

    n_pages = PAST_LEN // PAGE_SIZE
    n_used = DEC_BATCH * n_pages
    n_phys = (n_used * 5 + 3) // 4
    page_table = jax.random.permutation(next(ks), n_phys)[:n_used].reshape(DEC_BATCH, n_pages).astype(jnp.int32)
    a0 = jax.random.uniform(next(ks), (N_LRU_LAYERS, LRU_WIDTH), f32, 0.9, 0.999)
    s_lam = a0 ** (1.0 / LRU_C)
    lru_lam = jnp.log(s_lam) - jnp.log1p(-s_lam)
    return {
        'x_prompt': nrm((BATCH, SEQ, D)),
        'x_sample': nrm((DEC_BATCH, DEC_SEQ, D)),
        'c_prompt': nrm((BATCH, D)),
        'c_sample': nrm((DEC_BATCH, D)),
        'cache_mla_kv': nrm((N_MLA_LAYERS, n_phys, PAGE_SIZE, KV_ROW)),
        'page_table': page_table,
        'state_hgrn': nrm((N_HGRN_LAYERS, DEC_BATCH, HG_HEADS, HG_DK, HG_DV), 0.5),
        'state_lru_h': nrm((N_LRU_LAYERS, DEC_BATCH, LRU_WIDTH), 0.5),
        'state_lru_conv': nrm((N_LRU_LAYERS, DEC_BATCH, CONV_W - 1, LRU_WIDTH)),
        'ada_w': nrm((DEPTH, D, N_MOD * D), 0.5 * D ** -0.5),
        'ada_b': nrm((DEPTH, N_MOD * D), 0.02),
        'norm_pre': gain((DEPTH, 3, D)),
        'norm_post': gain((DEPTH, 3, D)),
        'ffn1_wg': nrm((DEPTH, D, D_FF), D ** -0.5),
        'ffn1_wu': nrm((DEPTH, D, D_FF), D ** -0.5),
        'ffn1_wd': nrm((DEPTH, D_FF, D), D_FF ** -0.5),
        'ffn2_wg': nrm((DEPTH, D, D_FF), D ** -0.5),
        'ffn2_wu': nrm((DEPTH, D, D_FF), D ** -0.5),
        'ffn2_wd': nrm((DEPTH, D_FF, D), D_FF ** -0.5),
        'mla_w_dq': nrm((N_MLA_LAYERS, D, Q_LORA), D ** -0.5),
        'mla_g_q': gain((N_MLA_LAYERS, Q_LORA)),
        'mla_w_uq': nrm((N_MLA_LAYERS, Q_LORA, MLA_HEADS * MLA_QK), Q_LORA ** -0.5),
        'mla_w_dkv': nrm((N_MLA_LAYERS, D, KV_ROW), D ** -0.5),
        'mla_g_kv': gain((N_MLA_LAYERS, KV_LORA)),
        'mla_w_uk': nrm((N_MLA_LAYERS, KV_LORA, MLA_HEADS, QK_NOPE), KV_LORA ** -0.5),
        'mla_w_uv': nrm((N_MLA_LAYERS, KV_LORA, MLA_HEADS, V_HEAD), KV_LORA ** -0.5),
        'mla_w_o': nrm((N_MLA_LAYERS, MLA_HEADS * V_HEAD, D), (MLA_HEADS * V_HEAD) ** -0.5),
        'hg_lb': nrm((DEPTH, HG_HEADS * HG_DK), 0.1),
        'hg_w_q': nrm((N_HGRN_LAYERS, D, HG_HEADS * HG_DK), D ** -0.5),
        'hg_w_f': nrm((N_HGRN_LAYERS, D, HG_HEADS * HG_DK), D ** -0.5),
        'hg_w_i': nrm((N_HGRN_LAYERS, D, HG_HEADS * HG_DV), D ** -0.5),
        'hg_w_g': nrm((N_HGRN_LAYERS, D, HG_HEADS * HG_DV), D ** -0.5),
        'hg_g_norm': gain((N_HGRN_LAYERS, HG_DV)),
        'hg_w_o': nrm((N_HGRN_LAYERS, HG_HEADS * HG_DV, D), D ** -0.5),
        'lru_w_x': nrm((N_LRU_LAYERS, D, LRU_WIDTH), D ** -0.5),
        'lru_w_y': nrm((N_LRU_LAYERS, D, LRU_WIDTH), D ** -0.5),
        'lru_conv_w': nrm((N_LRU_LAYERS, CONV_W, LRU_WIDTH), CONV_W ** -0.5),
        'lru_conv_b': nrm((N_LRU_LAYERS, LRU_WIDTH), 0.02),
        'lru_w_ra': nrm((N_LRU_LAYERS, LRU_BLOCKS, LRU_BW, LRU_BW), LRU_BW ** -0.5),
        'lru_b_ra': nrm((N_LRU_LAYERS, LRU_WIDTH), 0.02),
        'lru_w_ix': nrm((N_LRU_LAYERS, LRU_BLOCKS, LRU_BW, LRU_BW), LRU_BW ** -0.5),
        'lru_b_ix': nrm((N_LRU_LAYERS, LRU_WIDTH), 0.02),
        'lru_lam': lru_lam,
        'lru_w_o': nrm((N_LRU_LAYERS, LRU_WIDTH, D), LRU_WIDTH ** -0.5),
    }


def reference(x_prompt, x_sample, c_prompt, c_sample, cache_mla_kv, page_table, state_hgrn, state_lru_h,
              state_lru_conv, ada_w, ada_b, norm_pre, norm_post, ffn1_wg, ffn1_wu, ffn1_wd, ffn2_wg, ffn2_wu,
              ffn2_wd, mla_w_dq, mla_g_q, mla_w_uq, mla_w_dkv, mla_g_kv, mla_w_uk, mla_w_uv, mla_w_o, hg_lb,
              hg_w_q, hg_w_f, hg_w_i, hg_w_g, hg_g_norm, hg_w_o, lru_w_x, lru_w_y, lru_conv_w, lru_conv_b,
              lru_w_ra, lru_b_ra, lru_w_ix, lru_b_ix, lru_lam, lru_w_o):
    params = {
        'ada_w': ada_w, 'ada_b': ada_b, 'norm_pre': norm_pre, 'norm_post': norm_post,
        'ffn1_wg': ffn1_wg, 'ffn1_wu': ffn1_wu, 'ffn1_wd': ffn1_wd,
        'ffn2_wg': ffn2_wg, 'ffn2_wu': ffn2_wu, 'ffn2_wd': ffn2_wd,
        'mla_w_dq': mla_w_dq, 'mla_g_q': mla_g_q, 'mla_w_uq': mla_w_uq, 'mla_w_dkv': mla_w_dkv,
        'mla_g_kv': mla_g_kv, 'mla_w_uk': mla_w_uk, 'mla_w_uv': mla_w_uv, 'mla_w_o': mla_w_o,
        'hg_lb': hg_lb, 'hg_w_q': hg_w_q, 'hg_w_f': hg_w_f, 'hg_w_i': hg_w_i, 'hg_w_g': hg_w_g,
        'hg_g_norm': hg_g_norm, 'hg_w_o': hg_w_o,
        'lru_w_x': lru_w_x, 'lru_w_y': lru_w_y, 'lru_conv_w': lru_conv_w, 'lru_conv_b': lru_conv_b,
        'lru_w_ra': lru_w_ra, 'lru_b_ra': lru_b_ra, 'lru_w_ix': lru_w_ix, 'lru_b_ix': lru_b_ix,
        'lru_lam': lru_lam, 'lru_w_o': lru_w_o,
    }
    pos_prompt = jnp.arange(x_prompt.shape[1])
    pos_sample = PAST_LEN + jnp.arange(x_sample.shape[1])
    y_prompt, kv_p, hg_p, lh_p, lc_p = run_trunk(x_prompt, c_prompt, pos_prompt, None, None, None, None, None, params)
    y_sample, kv_s, hg_s, lh_s, lc_s = run_trunk(x_sample, c_sample, pos_sample, cache_mla_kv, page_table,
                                                 state_hgrn, state_lru_h, state_lru_conv, params)
    return (y_prompt, y_sample, kv_p, kv_s, hg_p, hg_s, lh_p, lh_s, lc_p, lc_s)
```

```python
import functools

import jax
import jax.numpy as jnp
from jax import lax
from jax.experimental import pallas as pl
from jax.experimental.pallas import tpu as pltpu

F32 = jnp.float32
BF16 = jnp.bfloat16

DEPTH = 4
N_MIXERS = 3
NORM_EPS = 1e-6
N_MOD = 9
FFN_RES = 0.5

MLA_HEADS = 16
Q_LORA = 512
KV_LORA = 512
QK_NOPE = 128
QK_ROPE = 64
V_HEAD = 128
KV_ROW = KV_LORA + QK_ROPE
KV_PAD = 640
QK_PAD = 256
MLA_SCALE = (QK_NOPE + QK_ROPE) ** -0.5
ROPE_THETA = 10000.0
PAGE_SIZE = 128
NEG_INF = -1e30

HG_HEADS = 16
HG_DK = 128
HG_DV = 128
HG_CHUNK = 16
HG_BLOCK = 128

LRU_BLOCKS = 16
LRU_BW = 128
CONV_W = 4
LRU_C = 8.0

LANE = 128
VMEM_LIMIT = 56 * 1024 * 1024


def _cparams(*sem):
    return pltpu.CompilerParams(dimension_semantics=sem, vmem_limit_bytes=VMEM_LIMIT)


def _dot(a, b):
    return jnp.dot(a, b, preferred_element_type=F32)


def _dot_nt(a, b):
    return lax.dot_general(a, b, (((1,), (1,)), ((), ())), preferred_element_type=F32)


def _rms(x, g):
    ms = jnp.mean(x * x, axis=-1, keepdims=True)
    return x * lax.rsqrt(ms + NORM_EPS) * g


def _sigmoid(x):
    return 1.0 / (1.0 + jnp.exp(-x))


def _silu(x):
    return x * _sigmoid(x)


def _softplus(x):
    return jnp.maximum(x, 0.0) + jnp.log1p(jnp.exp(-jnp.abs(x)))


def _gelu_tanh(x):
    c = 0.7978845608028654
    return 0.5 * x * (1.0 + jnp.tanh(c * (x + 0.044715 * (x * x * x))))


def _prenorm(x_ref, sh_ref, sc_ref, g_ref):
    return (_rms(x_ref[...], g_ref[...]) * (1.0 + sc_ref[...]) + sh_ref[...]).astype(BF16)


class Stream:
    def __init__(self, x, mod, per_row):
        self.x, self.mod, self.per_row = x, mod, per_row
        self.G, self.R, self.D = x.shape
        self.bm = min(512, self.R)

    def with_x(self, x):
        return Stream(x, self.mod, self.per_row)

    def rows(self, width):
        return pl.BlockSpec((None, self.bm, width), lambda g, i, *_: (g, i, 0))

    def mod_spec(self, li, k):
        G, per_row = self.G, self.per_row
        rm = self.bm if per_row else 1
        return pl.BlockSpec((None, rm, self.D), lambda g, i, *_: (li * G + g, i if per_row else 0, k))


def _gain_spec(idx, width):
    return pl.BlockSpec((None, 1, width), lambda *_: (idx, 0, 0))


def _ada_kernel(c_ref, w_ref, b_ref, o_ref):
    cs = _silu(c_ref[...]).astype(BF16)
    o_ref[...] = _dot(cs, w_ref[...].astype(BF16)) + b_ref[...]


def ada_mod(c_all, ada_w, ada_b):
    Bc, D = c_all.shape
    N = ada_w.shape[-1]
    bn = 1024
    return pl.pallas_call(
        _ada_kernel,
        grid=(DEPTH, N // bn),
        in_specs=[pl.BlockSpec((Bc, D), lambda l, j: (0, 0)),
                  pl.BlockSpec((None, D, bn), lambda l, j: (l, 0, j)),
                  pl.BlockSpec((None, 1, bn), lambda l, j: (l, 0, j))],
        out_specs=pl.BlockSpec((None, Bc, bn), lambda l, j: (l, 0, j)),
        out_shape=jax.ShapeDtypeStruct((DEPTH, Bc, N), F32),
        compiler_params=_cparams("parallel", "parallel"),
        name="ada_mod",
    )(c_all, ada_w, ada_b.reshape(DEPTH, 1, N))


def _ffn_kernel(x_ref, sh_ref, sc_ref, gt_ref, gpre_ref, gpost_ref, wg_ref, wu_ref, wd_ref, o_ref, h_ref, acc_ref):
    j = pl.program_id(2)

    @pl.when(j == 0)
    def _():
        h_ref[...] = _prenorm(x_ref, sh_ref, sc_ref, gpre_ref)
        acc_ref[...] = jnp.zeros_like(acc_ref)

    h = h_ref[...]
    g = _dot(h, wg_ref[...])
    u = _dot(h, wu_ref[...])
    acc_ref[...] += _dot((_silu(g) * u).astype(BF16), wd_ref[...])

    @pl.when(j == pl.num_programs(2) - 1)
    def _():
        o_ref[...] = x_ref[...] + FFN_RES * gt_ref[...] * _rms(acc_ref[...], gpost_ref[...])


def ffn(s, li, sub, wg, wu, wd, norm_pre, norm_post):
    D, DF = wg.shape[1], wg.shape[2]
    bf = 512
    k0 = 3 * sub
    return pl.pallas_call(
        _ffn_kernel,
        grid=(s.G, s.R // s.bm, DF // bf),
        in_specs=[s.rows(D), s.mod_spec(li, k0), s.mod_spec(li, k0 + 1), s.mod_spec(li, k0 + 2),
                  _gain_spec(li * 3 + sub, D), _gain_spec(li * 3 + sub, D),
                  pl.BlockSpec((None, D, bf), lambda g, i, j: (li, 0, j)),
                  pl.BlockSpec((None, D, bf), lambda g, i, j: (li, 0, j)),
                  pl.BlockSpec((None, bf, D), lambda g, i, j: (li, j, 0))],
        out_specs=s.rows(D),
        out_shape=jax.ShapeDtypeStruct(s.x.shape, F32),
        scratch_shapes=[pltpu.VMEM((s.bm, D), BF16), pltpu.VMEM((s.bm, D), F32)],
        compiler_params=_cparams("parallel", "parallel", "arbitrary"),
        name="ffn",
    )(s.x, s.mod, s.mod, s.mod, norm_pre, norm_post, wg, wu, wd)


def _mm_post_kernel(a_ref, w_ref, x_ref, gt_ref, gpost_ref, o_ref):
    y = _dot(a_ref[...], w_ref[...])
    o_ref[...] = x_ref[...] + gt_ref[...] * _rms(y, gpost_ref[...])


def mm_post(s, li, a, w, norm_post):
    K, D = w.shape
    bm = min(256, s.R)
    rows = lambda width: pl.BlockSpec((None, bm, width), lambda g, i: (g, i, 0))
    G, per_row = s.G, s.per_row
    gate = pl.BlockSpec((None, bm if per_row else 1, D), lambda g, i: (li * G + g, i if per_row else 0, 5))
    return pl.pallas_call(
        _mm_post_kernel,
        grid=(s.G, s.R // bm),
        in_specs=[rows(K), pl.BlockSpec((K, D), lambda g, i: (0, 0)), rows(D), gate, _gain_spec(li * 3 + 1, D)],
        out_specs=rows(D),
        out_shape=jax.ShapeDtypeStruct(s.x.shape, F32),
        compiler_params=_cparams("parallel", "parallel"),
        name="mm_post",
    )(a, w, s.x, s.mod, norm_post)


def _mla_proj_kernel(x_ref, sh_ref, sc_ref, gpre_ref, w_ref, gq_ref, gkv_ref, cs_ref, ql_ref, kv_ref, kvb_ref):
    h = _prenorm(x_ref, sh_ref, sc_ref, gpre_ref)
    r = _dot(h, w_ref[...])
    ql_ref[...] = _rms(r[:, :Q_LORA], gq_ref[...]).astype(BF16)
    ckv = _rms(r[:, Q_LORA:Q_LORA + KV_LORA], gkv_ref[...])
    t = r[:, Q_LORA + KV_LORA:] * cs_ref[...]
    kr = t[:, :QK_ROPE] + t[:, QK_ROPE:]
    kv_ref[:, :KV_LORA] = ckv
    kv_ref[:, KV_LORA:] = kr
    kvb_ref[:, :KV_LORA] = ckv.astype(BF16)
    kvb_ref[:, KV_LORA:] = jnp.concatenate([kr, jnp.zeros_like(kr)], axis=-1).astype(BF16)


def mla_proj(s, li, w_cat, g_q, g_kv, cs_tab, norm_pre):
    D, N = w_cat.shape
    out = (jax.ShapeDtypeStruct((s.G, s.R, Q_LORA), BF16),
           jax.ShapeDtypeStruct((s.G, s.R, KV_ROW), F32),
           jax.ShapeDtypeStruct((s.G, s.R, KV_PAD), BF16))
    return pl.pallas_call(
        _mla_proj_kernel,
        grid=(s.G, s.R // s.bm),
        in_specs=[s.rows(D), s.mod_spec(li, 3), s.mod_spec(li, 4), _gain_spec(li * 3 + 1, D),
                  pl.BlockSpec((D, N), lambda g, i: (0, 0)),
                  pl.BlockSpec((1, Q_LORA), lambda g, i: (0, 0)),
                  pl.BlockSpec((1, KV_LORA), lambda g, i: (0, 0)),
                  pl.BlockSpec((s.bm, LANE), lambda g, i: (i, 0))],
        out_specs=(s.rows(Q_LORA), s.rows(KV_ROW), s.rows(KV_PAD)),
        out_shape=out,
        compiler_params=_cparams("parallel", "parallel"),
        name="mla_proj",
    )(s.x, s.mod, s.mod, norm_pre, w_cat, g_q, g_kv, cs_tab)


def _q_expand_kernel(ql_ref, w_ref, wr_ref, cm_ref, sm_ref, q_ref, *, hb):
    ql = ql_ref[...]
    a = _dot(ql, w_ref[...])
    b = _dot(ql, wr_ref[...])
    cm, sm = cm_ref[...], sm_ref[...]
    for h in range(hb):
        sl = slice(h * QK_PAD, (h + 1) * QK_PAD)
        q_ref[:, sl] = (a[:, sl] * cm + b[:, sl] * sm).astype(BF16)


def q_expand(s, ql, w_q, w_qrot, cmap, smap):
    hb = 4
    nw = hb * QK_PAD
    wspec = pl.BlockSpec((Q_LORA, nw), lambda g, i, j: (0, j))
    tspec = pl.BlockSpec((s.bm, QK_PAD), lambda g, i, j: (i, 0))
    return pl.pallas_call(
        functools.partial(_q_expand_kernel, hb=hb),
        grid=(s.G, s.R // s.bm, MLA_HEADS // hb),
        in_specs=[s.rows(Q_LORA), wspec, wspec, tspec, tspec],
        out_specs=pl.BlockSpec((None, s.bm, nw), lambda g, i, j: (g, i, j)),
        out_shape=jax.ShapeDtypeStruct((s.G, s.R, MLA_HEADS * QK_PAD), BF16),
        compiler_params=_cparams("parallel", "parallel", "parallel"),
        name="q_expand",
    )(ql, w_q, w_qrot, cmap, smap)


def _kv_expand_kernel(kvb_ref, wk_ref, wv_ref, k_ref, v_ref):
    kvb = kvb_ref[...]
    k_ref[...] = _dot(kvb, wk_ref[...]).astype(BF16)
    v_ref[...] = _dot(kvb, wv_ref[...]).astype(BF16)


def kv_expand(s, kvb, w_k, w_v):
    hb = 4
    return pl.pallas_call(
        _kv_expand_kernel,
        grid=(s.G, s.R // s.bm, MLA_HEADS // hb),
        in_specs=[s.rows(KV_PAD),
                  pl.BlockSpec((KV_PAD, hb * QK_PAD), lambda g, i, j: (0, j)),
                  pl.BlockSpec((KV_PAD, hb * V_HEAD), lambda g, i, j: (0, j))],
        out_specs=(pl.BlockSpec((None, s.bm, hb * QK_PAD), lambda g, i, j: (g, i, j)),
                   pl.BlockSpec((None, s.bm, hb * V_HEAD), lambda g, i, j: (g, i, j))),
        out_shape=(jax.ShapeDtypeStruct((s.G, s.R, MLA_HEADS * QK_PAD), BF16),
                   jax.ShapeDtypeStruct((s.G, s.R, MLA_HEADS * V_HEAD), BF16)),
        compiler_params=_cparams("parallel", "parallel", "parallel"),
        name="kv_expand",
    )(kvb, w_k, w_v)


def _flash_kernel(q_ref, k_ref, v_ref, o_ref, m_ref, l_ref, acc_ref, *, blk):
    i, j = pl.program_id(2), pl.program_id(3)

    @pl.when(j == 0)
    def _():
        m_ref[...] = jnp.full_like(m_ref, NEG_INF)
        l_ref[...] = jnp.zeros_like(l_ref)
        acc_ref[...] = jnp.zeros_like(acc_ref)

    def step(diagonal):
        s = _dot_nt(q_ref[...], k_ref[...])
        if diagonal:
            row = lax.broadcasted_iota(jnp.int32, (blk, blk), 0)
            col = lax.broadcasted_iota(jnp.int32, (blk, blk), 1)
            s = jnp.where(col <= row, s, NEG_INF)
        m_prev = m_ref[...]
        m_new = jnp.maximum(m_prev, jnp.max(s, axis=-1, keepdims=True))
        alpha = jnp.exp(m_prev - m_new)
        p = jnp.exp(s - m_new)
        l_ref[...] = alpha * l_ref[...] + jnp.sum(p, axis=-1, keepdims=True)
        acc_ref[...] = alpha * acc_ref[...] + _dot(p.astype(BF16), v_ref[...])
        m_ref[...] = m_new

    @pl.when(j < i)
    def _():
        step(False)

    @pl.when(j == i)
    def _():
        step(True)
        o_ref[...] = (acc_ref[...] / l_ref[...]).astype(o_ref.dtype)


def flash_prompt(q, k, v):
    B, L, _ = q.shape
    blk = min(512, L)
    n = L // blk
    return pl.pallas_call(
        functools.partial(_flash_kernel, blk=blk),
        grid=(B, MLA_HEADS, n, n),
        in_specs=[pl.BlockSpec((None, blk, QK_PAD), lambda b, h, i, j: (b, i, h)),
                  pl.BlockSpec((None, blk, QK_PAD), lambda b, h, i, j: (b, jnp.minimum(i, j), h)),
                  pl.BlockSpec((None, blk, V_HEAD), lambda b, h, i, j: (b, jnp.minimum(i, j), h))],
        out_specs=pl.BlockSpec((None, blk, V_HEAD), lambda b, h, i, j: (b, i, h)),
        out_shape=jax.ShapeDtypeStruct((B, L, MLA_HEADS * V_HEAD), BF16),
        scratch_shapes=[pltpu.VMEM((blk, 1), F32), pltpu.VMEM((blk, 1), F32), pltpu.VMEM((blk, V_HEAD), F32)],
        compiler_params=_cparams("parallel", "parallel", "parallel", "arbitrary"),
        name="flash_prompt",
    )(q, k, v)


def _q_absorb_kernel(q_ref, w_ref, qc_ref):
    q = q_ref[...]
    qc_ref[:, :KV_LORA] = _dot(q[:, :QK_NOPE], w_ref[...]).astype(BF16)
    qc_ref[:, KV_LORA:] = q[:, QK_NOPE:]


def q_absorb(q, w_ukT):
    R = q.shape[0]
    return pl.pallas_call(
        _q_absorb_kernel,
        grid=(MLA_HEADS,),
        in_specs=[pl.BlockSpec((R, QK_PAD), lambda h: (0, h)),
                  pl.BlockSpec((None, QK_NOPE, KV_LORA), lambda h: (h, 0, 0))],
        out_specs=pl.BlockSpec((None, R, KV_PAD), lambda h: (h, 0, 0)),
        out_shape=jax.ShapeDtypeStruct((MLA_HEADS, R, KV_PAD), BF16),
        compiler_params=_cparams("parallel"),
        name="q_absorb",
    )(q, w_ukT)


def _paged_kernel(pt_ref, qc_ref, new_ref, *rest, pp, heads):
    pages = rest[:pp]
    o_ref, kb_ref, kn_ref, m_ref, l_ref, acc_ref = rest[pp:]
    j = pl.program_id(1)
    rows = qc_ref.shape[0]

    @pl.when(j == 0)
    def _():
        m_ref[...] = jnp.full_like(m_ref, NEG_INF)
        l_ref[...] = jnp.zeros_like(l_ref)
        acc_ref[...] = jnp.zeros_like(acc_ref)
        kb_ref[:, KV_ROW:] = jnp.zeros((kb_ref.shape[0], KV_PAD - KV_ROW), BF16)

    def update(keys, visible):
        s = _dot_nt(qc_ref[...], keys)
        if visible is not None:
            s = jnp.where(visible, s, NEG_INF)
        m_prev = m_ref[...]
        m_new = jnp.maximum(m_prev, jnp.max(s, axis=-1, keepdims=True))
        alpha = jnp.exp(m_prev - m_new)
        p = jnp.exp(s - m_new)
        l_ref[...] = alpha * l_ref[...] + jnp.sum(p, axis=-1, keepdims=True)
        acc_ref[...] = alpha * acc_ref[...] + _dot(p.astype(BF16), keys[:, :KV_LORA])
        m_ref[...] = m_new

    for k in range(pp):
        kb_ref[k * PAGE_SIZE:(k + 1) * PAGE_SIZE, :KV_ROW] = pages[k][...].astype(BF16)
    update(kb_ref[...], None)

    @pl.when(j == pl.num_programs(1) - 1)
    def _():
        T = new_ref.shape[0]
        kn_ref[...] = jnp.zeros_like(kn_ref)
        kn_ref[:T, :KV_ROW] = new_ref[...].astype(BF16)
        row = lax.broadcasted_iota(jnp.int32, (rows, PAGE_SIZE), 0)
        col = lax.broadcasted_iota(jnp.int32, (rows, PAGE_SIZE), 1)
        update(kn_ref[...], col * heads <= row)
        o_ref[...] = acc_ref[...] / l_ref[...]


def paged_attention(qc, kv_new, cache, layer, page_table):
    Bd, rows, _ = qc.shape
    T = kv_new.shape[1]
    n_pages = page_table.shape[1]
    pp = 16 if n_pages % 16 == 0 else n_pages
    page_specs = [pl.BlockSpec((None, None, PAGE_SIZE, KV_ROW),
                               functools.partial(lambda b, j, pt, k: (layer, pt[b, j * pp + k], 0, 0), k=k))
                  for k in range(pp)]
    grid_spec = pltpu.PrefetchScalarGridSpec(
        num_scalar_prefetch=1,
        grid=(Bd, n_pages // pp),
        in_specs=[pl.BlockSpec((None, rows, KV_PAD), lambda b, j, pt: (b, 0, 0)),
                  pl.BlockSpec((None, T, KV_ROW), lambda b, j, pt: (b, 0, 0))] + page_specs,
        out_specs=pl.BlockSpec((None, rows, KV_LORA), lambda b, j, pt: (b, 0, 0)),
        scratch_shapes=[pltpu.VMEM((pp * PAGE_SIZE, KV_PAD), BF16), pltpu.VMEM((PAGE_SIZE, KV_PAD), BF16),
                        pltpu.VMEM((rows, 1), F32), pltpu.VMEM((rows, 1), F32), pltpu.VMEM((rows, KV_LORA), F32)])
    return pl.pallas_call(
        functools.partial(_paged_kernel, pp=pp, heads=rows // T),
        grid_spec=grid_spec,
        out_shape=jax.ShapeDtypeStruct((Bd, rows, KV_LORA), F32),
        compiler_params=_cparams("parallel", "arbitrary"),
        name="paged_attention",
    )(page_table, qc, kv_new, *([cache] * pp))


def _v_up_kernel(o_ref, w_ref, y_ref):
    y_ref[...] = _dot(o_ref[...].astype(BF16), w_ref[...]).astype(BF16)


def v_up(o_lat, w_uv):
    R = o_lat.shape[0]
    return pl.pallas_call(
        _v_up_kernel,
        grid=(MLA_HEADS,),
        in_specs=[pl.BlockSpec((R, KV_LORA), lambda h: (0, h)), pl.BlockSpec((KV_LORA, V_HEAD), lambda h: (0, h))],
        out_specs=pl.BlockSpec((R, V_HEAD), lambda h: (0, h)),
        out_shape=jax.ShapeDtypeStruct((R, MLA_HEADS * V_HEAD), BF16),
        compiler_params=_cparams("parallel"),
        name="v_up",
    )(o_lat, w_uv)


def _hg_proj_kernel(x_ref, sh_ref, sc_ref, gpre_ref, wq_ref, wf_ref, wi_ref, wg_ref, lb_ref,
                    q_ref, k_ref, v_ref, lf_ref, g_ref, h_ref):
    @pl.when(pl.program_id(2) == 0)
    def _():
        h_ref[...] = _prenorm(x_ref, sh_ref, sc_ref, gpre_ref)

    h = h_ref[...]
    q_ref[...] = _silu(_dot(h, wq_ref[...]))
    fz = _dot(h, wf_ref[...])
    lb = lb_ref[...]
    a = jnp.log(lb)
    b = jnp.log1p(-lb) + (jnp.minimum(fz, 0.0) - jnp.log1p(jnp.exp(-jnp.abs(fz))))
    lf_ref[...] = jnp.maximum(a, b) + jnp.log1p(jnp.exp(-jnp.abs(a - b)))
    k_ref[...] = (1.0 - lb) * _sigmoid(-fz)
    v_ref[...] = _dot(h, wi_ref[...])
    g_ref[...] = _silu(_dot(h, wg_ref[...]))


def hg_proj(s, li, w_q, w_f, w_i, w_g, lb, norm_pre):
    D, N = w_q.shape
    bn = 512
    wspec = pl.BlockSpec((D, bn), lambda g, i, j: (0, j))
    ospec = pl.BlockSpec((None, s.bm, bn), lambda g, i, j: (g, i, j))
    oshape = jax.ShapeDtypeStruct((s.G, s.R, N), F32)
    return pl.pallas_call(
        _hg_proj_kernel,
        grid=(s.G, s.R // s.bm, N // bn),
        in_specs=[s.rows(D), s.mod_spec(li, 3), s.mod_spec(li, 4), _gain_spec(li * 3 + 1, D),
                  wspec, wspec, wspec, wspec, pl.BlockSpec((1, bn), lambda g, i, j: (0, j))],
        out_specs=(ospec,) * 5,
        out_shape=(oshape,) * 5,
        scratch_shapes=[pltpu.VMEM((s.bm, D), BF16)],
        compiler_params=_cparams("parallel", "parallel", "arbitrary"),
        name="hg_proj",
    )(s.x, s.mod, s.mod, norm_pre, w_q, w_f, w_i, w_g, lb)


def _cumsum_rows(x, span):
    row = lax.broadcasted_iota(jnp.int32, x.shape, 0)
    pos = jnp.bitwise_and(row, span - 1)
    sh = 1
    while sh < span:
        x = x + jnp.where(pos >= sh, pltpu.roll(x, sh, 0), 0.0)
        sh *= 2
    return x


def _chunk_id(r):
    return lax.shift_right_logical(r, HG_CHUNK.bit_length() - 1)


def _hg_scan_kernel(q_ref, k_ref, v_ref, lf_ref, g_ref, s0_ref, gn_ref, o_ref, so_ref, st_ref, *, hb, chained):
    n = HG_BLOCK // HG_CHUNK
    first = pl.program_id(2) == 0
    last = pl.program_id(2) == pl.num_programs(2) - 1

    if chained:
        @pl.when(first)
        def _():
            for h in range(hb):
                st_ref[h] = s0_ref[h].T

    row = lax.broadcasted_iota(jnp.int32, (HG_BLOCK, HG_BLOCK), 0)
    col = lax.broadcasted_iota(jnp.int32, (HG_BLOCK, HG_BLOCK), 1)
    causal = (_chunk_id(row) == _chunk_id(col)) & (col <= row)
    chunk_of_row = _chunk_id(lax.broadcasted_iota(jnp.int32, (HG_BLOCK, HG_DK), 0))

    for h in range(hb):
        sl = slice(h * HG_DK, (h + 1) * HG_DK)
        b = _cumsum_rows(lf_ref[:, sl], HG_CHUNK)
        b_last = jnp.concatenate(
            [jnp.broadcast_to(b[c * HG_CHUNK + HG_CHUNK - 1:(c + 1) * HG_CHUNK, :], (HG_CHUNK, HG_DK)) for c in range(n)],
            axis=0)
        q, k, v = q_ref[:, sl], k_ref[:, sl], v_ref[:, sl]
        qg = q * jnp.exp(b)
        kd = (k * jnp.exp(-b)).astype(BF16)
        kl = k * jnp.exp(b_last - b)
        qgb = qg.astype(BF16)
        vb = v.astype(BF16)
        a = jnp.where(causal, _dot_nt(qgb, kd), 0.0)
        o = _dot(a.astype(BF16), vb)
        klx = jnp.concatenate([jnp.where(chunk_of_row == c, kl, 0.0).astype(BF16) for c in range(n)], axis=1)
        ut = _dot(v.T.astype(BF16), klx)
        sts = []
        if chained:
            st = st_ref[h]
        for c in range(n):
            if not chained:
                st = s0_ref[c, h].T
            sts.append(st.astype(BF16))
            dec = jnp.exp(b_last[c * HG_CHUNK:c * HG_CHUNK + 1, :])
            st = st * dec + ut[:, c * HG_DK:(c + 1) * HG_DK]
            if not chained:
                so_ref[c, h] = st.T
        if chained:
            st_ref[h] = st

            @pl.when(last)
            def _():
                so_ref[h] = st.T
        qx = jnp.concatenate([jnp.where(chunk_of_row == c, qg, 0.0).astype(BF16) for c in range(n)], axis=1)
        o = o + _dot_nt(qx, jnp.concatenate(sts, axis=1))
        o_ref[:, sl] = (_rms(o, gn_ref[...]) * g_ref[:, sl]).astype(BF16)


def hg_scan(q, k, v, lf, gate, s0, g_norm, chained):
    G, R, N = q.shape
    hb = 2
    n = HG_BLOCK // HG_CHUNK
    rows = pl.BlockSpec((None, HG_BLOCK, hb * HG_DK), lambda g, h, l: (g, l, h))
    if chained:
        sspec = pl.BlockSpec((None, hb, HG_DK, HG_DV), lambda g, h, l: (g, h, 0, 0))
    else:
        sspec = pl.BlockSpec((None, n, hb, HG_DK, HG_DV), lambda g, h, l: (g, 0, h, 0, 0))
    return pl.pallas_call(
        functools.partial(_hg_scan_kernel, hb=hb, chained=chained),
        grid=(G, HG_HEADS // hb, R // HG_BLOCK),
        in_specs=[rows, rows, rows, rows, rows, sspec, pl.BlockSpec((1, HG_DV), lambda g, h, l: (0, 0))],
        out_specs=(rows, sspec),
        out_shape=(jax.ShapeDtypeStruct((G, R, N), BF16), jax.ShapeDtypeStruct(s0.shape, F32)),
        scratch_shapes=[pltpu.VMEM((hb, HG_DV, HG_DK), F32)],
        compiler_params=_cparams("parallel", "parallel", "arbitrary"),
        name="hg_scan",
    )(q, k, v, lf, gate, s0, g_norm)


def _lru_proj_kernel(x_ref, sh_ref, sc_ref, gpre_ref, wx_ref, wy_ref, u_ref, g_ref, h_ref):
    @pl.when(pl.program_id(2) == 0)
    def _():
        h_ref[...] = _prenorm(x_ref, sh_ref, sc_ref, gpre_ref)

    h = h_ref[...]
    u_ref[...] = _dot(h, wx_ref[...])
    g_ref[...] = _gelu_tanh(_dot(h, wy_ref[...]))


def lru_proj(s, li, w_x, w_y, norm_pre):
    D, N = w_x.shape
    bn = 512
    wspec = pl.BlockSpec((D, bn), lambda g, i, j: (0, j))
    ospec = pl.BlockSpec((None, s.bm, bn), lambda g, i, j: (g, i, j))
    oshape = jax.ShapeDtypeStruct((s.G, s.R, N), F32)
    return pl.pallas_call(
        _lru_proj_kernel,
        grid=(s.G, s.R // s.bm, N // bn),
        in_specs=[s.rows(D), s.mod_spec(li, 3), s.mod_spec(li, 4), _gain_spec(li * 3 + 1, D), wspec, wspec],
        out_specs=(ospec, ospec),
        out_shape=(oshape, oshape),
        scratch_shapes=[pltpu.VMEM((s.bm, D), BF16)],
        compiler_params=_cparams("parallel", "parallel", "arbitrary"),
        name="lru_proj",
    )(s.x, s.mod, s.mod, norm_pre, w_x, w_y)


def _lru_gates(xc, n, wra_ref, bra_ref, wix_ref, bix_ref, lam_ref):
    sl = slice(n * LRU_BW, (n + 1) * LRU_BW)
    xb = xc.astype(BF16)
    r = _sigmoid(_dot(xb, wra_ref[n]) + bra_ref[:, sl])
    ig = _sigmoid(_dot(xb, wix_ref[n]) + bix_ref[:, sl])
    log_a = -LRU_C * r * _softplus(-lam_ref[:, sl])
    a = jnp.exp(log_a)
    return a, jnp.sqrt(-jnp.tanh(log_a) * (a * a + 1.0)) * (ig * xc)


def _lru_scan_kernel(u_ref, gate_ref, cb_ref, h0_ref, cw_ref, cbias_ref, wra_ref, bra_ref, wix_ref, bix_ref, lam_ref,
                     yg_ref, hl_ref, cbo_ref, ext_ref, a_ref, b_ref, hc_ref, *, bl):
    l = pl.program_id(1)
    keep = CONV_W - 1
    base = 8

    @pl.when(l == 0)
    def _():
        ext_ref[base - keep:base, :] = cb_ref[...]
        hc_ref[...] = h0_ref[...]

    ext_ref[base:base + bl, :] = u_ref[...]
    for n in range(LRU_BLOCKS):
        sl = slice(n * LRU_BW, (n + 1) * LRU_BW)
        xc = cbias_ref[:, sl]
        for j in range(CONV_W):
            xc = xc + ext_ref[base - keep + j:base - keep + j + bl, sl] * cw_ref[j:j + 1, sl]
        a, b = _lru_gates(xc, n, wra_ref, bra_ref, wix_ref, bix_ref, lam_ref)
        a_ref[:, sl] = a
        b_ref[:, sl] = b

    pos = lax.broadcasted_iota(jnp.int32, (8, a_ref.shape[1]), 0)

    def tile(t, h_prev):
        r0 = pl.multiple_of(t * 8, 8)
        a_t = a_ref[pl.ds(r0, 8), :]
        b_t = b_ref[pl.ds(r0, 8), :]
        for sh in (1, 2, 4):
            m = pos >= sh
            b_t = jnp.where(m, a_t * pltpu.roll(b_t, sh, 0) + b_t, b_t)
            a_t = jnp.where(m, a_t * pltpu.roll(a_t, sh, 0), a_t)
        h_t = b_t + a_t * h_prev
        b_ref[pl.ds(r0, 8), :] = h_t
        return h_t[7:8, :]

    h_last = lax.fori_loop(0, bl // 8, tile, hc_ref[...])
    hc_ref[...] = h_last
    yg_ref[...] = (b_ref[...] * gate_ref[...]).astype(BF16)
    tail = ext_ref[base + bl - keep:base + bl, :]
    ext_ref[base - keep:base, :] = tail

    @pl.when(l == pl.num_programs(1) - 1)
    def _():
        hl_ref[...] = h_last
        cbo_ref[...] = tail


def lru_scan(u, gate, conv_buf, h0, conv_w, conv_b, w_ra, b_ra, w_ix, b_ix, lam):
    B, L, W = u.shape
    bl = min(256, L)
    rows = pl.BlockSpec((None, bl, W), lambda b, l: (b, l, 0))
    vec = pl.BlockSpec((1, W), lambda b, l: (0, 0))
    wblk = pl.BlockSpec((LRU_BLOCKS, LRU_BW, LRU_BW), lambda b, l: (0, 0, 0))
    hspec = pl.BlockSpec((None, 1, W), lambda b, l: (b, 0, 0))
    cspec = pl.BlockSpec((None, CONV_W - 1, W), lambda b, l: (b, 0, 0))
    return pl.pallas_call(
        functools.partial(_lru_scan_kernel, bl=bl),
        grid=(B, L // bl),
        in_specs=[rows, rows, cspec, hspec, pl.BlockSpec((CONV_W, W), lambda b, l: (0, 0)), vec,
                  wblk, vec, wblk, vec, vec],
        out_specs=(rows, hspec, cspec),
        out_shape=(jax.ShapeDtypeStruct((B, L, W), BF16), jax.ShapeDtypeStruct((B, 1, W), F32),
                   jax.ShapeDtypeStruct((B, CONV_W - 1, W), F32)),
        scratch_shapes=[pltpu.VMEM((bl + 8, W), F32), pltpu.VMEM((bl, W), F32), pltpu.VMEM((bl, W), F32),
                        pltpu.VMEM((1, W), F32)],
        compiler_params=_cparams("parallel", "arbitrary"),
        name="lru_scan",
    )(u, gate, conv_buf, h0.reshape(B, 1, W), conv_w, conv_b, w_ra, b_ra, w_ix, b_ix, lam)


def _lru_step_kernel(u_ref, gate_ref, cb_ref, h0_ref, cw_ref, cbias_ref, wra_ref, bra_ref, wix_ref, bix_ref, lam_ref,
                     yg_ref, hl_ref, cbo_ref):
    T = u_ref.shape[0]
    keep = CONV_W - 1
    for n in range(LRU_BLOCKS):
        sl = slice(n * LRU_BW, (n + 1) * LRU_BW)
        ext = [cb_ref[j, :, sl] for j in range(keep)] + [u_ref[t, :, sl] for t in range(T)]
        h = h0_ref[:, sl]
        for t in range(T):
            xc = cbias_ref[:, sl]
            for j in range(CONV_W):
                xc = xc + ext[t + j] * cw_ref[j:j + 1, sl]
            a, b = _lru_gates(xc, n, wra_ref, bra_ref, wix_ref, bix_ref, lam_ref)
            h = a * h + b
            yg_ref[t, :, sl] = (h * gate_ref[t, :, sl]).astype(BF16)
        hl_ref[:, sl] = h
        for j in range(keep):
            cbo_ref[j, :, sl] = ext[T + j]


def lru_step(u, gate, conv_buf, h0, conv_w, conv_b, w_ra, b_ra, w_ix, b_ix, lam):
    T, Bd, W = u.shape
    full = lambda *shape: pl.BlockSpec(shape, lambda i: (0,) * len(shape))
    return pl.pallas_call(
        _lru_step_kernel,
        grid=(1,),
        in_specs=[full(T, Bd, W), full(T, Bd, W), full(CONV_W - 1, Bd, W), full(Bd, W), full(CONV_W, W), full(1, W),
                  full(LRU_BLOCKS, LRU_BW, LRU_BW), full(1, W), full(LRU_BLOCKS, LRU_BW, LRU_BW), full(1, W), full(1, W)],
        out_specs=(full(T, Bd, W), full(Bd, W), full(CONV_W - 1, Bd, W)),
        out_shape=(jax.ShapeDtypeStruct((T, Bd, W), BF16), jax.ShapeDtypeStruct((Bd, W), F32),
                   jax.ShapeDtypeStruct((CONV_W - 1, Bd, W), F32)),
        compiler_params=_cparams("arbitrary"),
        name="lru_step",
    )(u, gate, conv_buf, h0, conv_w, conv_b, w_ra, b_ra, w_ix, b_ix, lam)


def _rope_tables(pos):
    half = QK_ROPE // 2
    inv = ROPE_THETA ** (-jnp.arange(half, dtype=F32) / half)
    ang = pos.astype(F32)[:, None] * inv[None, :]
    cos, sin = jnp.cos(ang), jnp.sin(ang)
    cos2, sin2 = jnp.concatenate([cos, cos], -1), jnp.concatenate([sin, sin], -1)
    ones, zeros = jnp.ones((pos.shape[0], QK_NOPE), F32), jnp.zeros((pos.shape[0], QK_NOPE), F32)
    pad = jnp.zeros((pos.shape[0], QK_PAD - QK_NOPE - QK_ROPE), F32)
    cs_tab = jnp.concatenate([cos2, sin2], -1)
    cmap = jnp.concatenate([ones, cos2, pad], -1) * MLA_SCALE
    smap = jnp.concatenate([zeros, sin2, pad], -1) * MLA_SCALE
    return cs_tab, cmap, smap


def _rot_cols(w):
    half = w.shape[-1] // 2
    return jnp.concatenate([-w[..., half:], w[..., :half]], -1)


def _mla_weights(w_dq, w_uq, w_dkv, w_uk, w_uv):
    kr = w_dkv[:, KV_LORA:]
    w_cat = jnp.concatenate([w_dq, w_dkv, _rot_cols(kr)], axis=1).astype(BF16)
    uq = w_uq.reshape(Q_LORA, MLA_HEADS, QK_NOPE + QK_ROPE)
    nope, rope = uq[..., :QK_NOPE], uq[..., QK_NOPE:]
    pad = jnp.zeros((Q_LORA, MLA_HEADS, QK_PAD - QK_NOPE - QK_ROPE), F32)
    w_q = jnp.concatenate([nope, rope, pad], -1).reshape(Q_LORA, MLA_HEADS * QK_PAD).astype(BF16)
    w_qrot = jnp.concatenate([jnp.zeros_like(nope), _rot_cols(rope), pad], -1)
    w_qrot = w_qrot.reshape(Q_LORA, MLA_HEADS * QK_PAD).astype(BF16)
    w_k = jnp.zeros((KV_PAD, MLA_HEADS, QK_PAD), F32)
    w_k = w_k.at[:KV_LORA, :, :QK_NOPE].set(w_uk)
    eye = jnp.broadcast_to(jnp.eye(QK_ROPE, dtype=F32)[:, None, :], (QK_ROPE, MLA_HEADS, QK_ROPE))
    w_k = w_k.at[KV_LORA:KV_ROW, :, QK_NOPE:QK_NOPE + QK_ROPE].set(eye)
    w_k = w_k.reshape(KV_PAD, MLA_HEADS * QK_PAD).astype(BF16)
    w_v = jnp.zeros((KV_PAD, MLA_HEADS * V_HEAD), F32).at[:KV_LORA].set(w_uv.reshape(KV_LORA, -1)).astype(BF16)
    w_ukT = w_uk.transpose(1, 2, 0).astype(BF16)
    w_uv2 = w_uv.reshape(KV_LORA, MLA_HEADS * V_HEAD).astype(BF16)
    return w_cat, w_q, w_qrot, w_k, w_v, w_ukT, w_uv2


def _hg_lower_bounds(lb_param):
    cs = jnp.cumsum(jax.nn.softmax(lb_param.astype(F32), axis=0), axis=0)
    return cs - cs[0]


def kernel(x_prompt, x_sample, c_prompt, c_sample, cache_mla_kv, page_table, state_hgrn, state_lru_h, state_lru_conv, ada_w, ada_b, norm_pre, norm_post, ffn1_wg, ffn1_wu, ffn1_wd, ffn2_wg, ffn2_wu, ffn2_wd, mla_w_dq, mla_g_q, mla_w_uq, mla_w_dkv, mla_g_kv, mla_w_uk, mla_w_uv, mla_w_o, hg_lb, hg_w_q, hg_w_f, hg_w_i, hg_w_g, hg_g_norm, hg_w_o, lru_w_x, lru_w_y, lru_conv_w, lru_conv_b, lru_w_ra, lru_b_ra, lru_w_ix, lru_b_ix, lru_lam, lru_w_o):
    B, L, D = x_prompt.shape
    Bd, T, _ = x_sample.shape
    n_pages = page_table.shape[1]
    past_len = n_pages * PAGE_SIZE

    n_c = B + Bd
    c_all = jnp.concatenate([c_prompt, c_sample, jnp.zeros((-n_c % 8, D), F32)], axis=0)
    mod = ada_mod(c_all, ada_w, ada_b)
    mod_p = mod[:, :B].reshape(DEPTH * B, 1, N_MOD * D)
    mod_s = jnp.repeat(mod[:, B:n_c], T, axis=1)
    sp = Stream(x_prompt, mod_p, per_row=False)
    ss = Stream(x_sample.reshape(1, Bd * T, D), mod_s, per_row=True)

    npre = norm_pre.reshape(DEPTH * 3, 1, D)
    npost = norm_post.reshape(DEPTH * 3, 1, D)
    bf = lambda w: w.astype(BF16)
    f1 = (bf(ffn1_wg), bf(ffn1_wu), bf(ffn1_wd))
    f2 = (bf(ffn2_wg), bf(ffn2_wu), bf(ffn2_wd))

    tab_p = _rope_tables(jnp.arange(L))
    tab_s = _rope_tables(jnp.tile(past_len + jnp.arange(T), Bd))
    lbs = _hg_lower_bounds(hg_lb)

    kv_p, kv_s, hg_p, hg_s, lh_p, lh_s, lc_p, lc_s = [], [], [], [], [], [], [], []
    for li in range(DEPTH):
        sp = sp.with_x(ffn(sp, li, 0, *f1, npre, npost))
        ss = ss.with_x(ffn(ss, li, 0, *f1, npre, npost))
        kind, j = li % N_MIXERS, li // N_MIXERS
        if kind == 0:
            w_cat, w_q, w_qrot, w_k, w_v, w_ukT, w_uv2 = _mla_weights(
                mla_w_dq[j], mla_w_uq[j], mla_w_dkv[j], mla_w_uk[j], mla_w_uv[j])
            g_q, g_kv = mla_g_q[j].reshape(1, Q_LORA), mla_g_kv[j].reshape(1, KV_LORA)
            w_o = bf(mla_w_o[j])
            ql, kvr, kvb = mla_proj(sp, li, w_cat, g_q, g_kv, tab_p[0], npre)
            q = q_expand(sp, ql, w_q, w_qrot, tab_p[1], tab_p[2])
            kk, vv = kv_expand(sp, kvb, w_k, w_v)
            o = flash_prompt(q, kk, vv)
            sp = sp.with_x(mm_post(sp, li, o, w_o, npost))
            kv_p.append(kvr)
            ql, kvr, _ = mla_proj(ss, li, w_cat, g_q, g_kv, tab_s[0], npre)
            q = q_expand(ss, ql, w_q, w_qrot, tab_s[1], tab_s[2])
            qc = q_absorb(q[0], w_ukT)
            qc = qc.reshape(MLA_HEADS, Bd, T, KV_PAD).transpose(1, 2, 0, 3).reshape(Bd, T * MLA_HEADS, KV_PAD)
            kv_new = kvr.reshape(Bd, T, KV_ROW)
            o_lat = paged_attention(qc, kv_new, cache_mla_kv, j, page_table)
            o = v_up(o_lat.reshape(Bd * T, MLA_HEADS * KV_LORA), w_uv2)
            ss = ss.with_x(mm_post(ss, li, o.reshape(1, Bd * T, -1), w_o, npost))
            kv_s.append(kv_new)
        elif kind == 1:
            lb = lbs[li].reshape(1, -1)
            ws = (bf(hg_w_q[j]), bf(hg_w_f[j]), bf(hg_w_i[j]), bf(hg_w_g[j]))
            g_n = hg_g_norm[j].reshape(1, HG_DV)
            w_o = bf(hg_w_o[j])
            q, k, v, lf, gate = hg_proj(sp, li, *ws, lb, npre)
            s0 = jnp.zeros((B, HG_HEADS, HG_DK, HG_DV), F32)
            o, s_new = hg_scan(q, k, v, lf, gate, s0, g_n, chained=True)
            sp = sp.with_x(mm_post(sp, li, o, w_o, npost))
            hg_p.append(s_new)
            n_seq = HG_BLOCK // HG_CHUNK
            def to_chunks(a):
                a = jnp.pad(a.reshape(Bd, T, -1), ((0, 0), (HG_CHUNK - T, 0), (0, 0)))
                return a.reshape(Bd // n_seq, HG_BLOCK, -1)
            qs, ks, vs, lfs, gs = [to_chunks(a) for a in hg_proj(ss, li, *ws, lb, npre)]
            s0 = state_hgrn[j].reshape(Bd // n_seq, n_seq, HG_HEADS, HG_DK, HG_DV)
            o, s_new = hg_scan(qs, ks, vs, lfs, gs, s0, g_n, chained=False)
            o = o.reshape(Bd, HG_CHUNK, -1)[:, HG_CHUNK - T:].reshape(1, Bd * T, -1)
            ss = ss.with_x(mm_post(ss, li, o, w_o, npost))
            hg_s.append(s_new.reshape(Bd, HG_HEADS, HG_DK, HG_DV))
        else:
            w_x, w_y, w_o = bf(lru_w_x[j]), bf(lru_w_y[j]), bf(lru_w_o[j])
            W = w_x.shape[1]
            prm = (lru_conv_w[j], lru_conv_b[j].reshape(1, W), bf(lru_w_ra[j]), lru_b_ra[j].reshape(1, W),
                   bf(lru_w_ix[j]), lru_b_ix[j].reshape(1, W), lru_lam[j].reshape(1, W))
            u, gate = lru_proj(sp, li, w_x, w_y, npre)
            yg, h_last, cb = lru_scan(u, gate, jnp.zeros((B, CONV_W - 1, W), F32), jnp.zeros((B, W), F32), *prm)
            sp = sp.with_x(mm_post(sp, li, yg, w_o, npost))
            lh_p.append(h_last.reshape(B, W))
            lc_p.append(cb)
            u, gate = lru_proj(ss, li, w_x, w_y, npre)
            tm = lambda a: a.reshape(Bd, T, W).transpose(1, 0, 2)
            yg, h_last, cb = lru_step(tm(u), tm(gate), state_lru_conv[j].transpose(1, 0, 2), state_lru_h[j], *prm)
            ss = ss.with_x(mm_post(ss, li, yg.transpose(1, 0, 2).reshape(1, Bd * T, W), w_o, npost))
            lh_s.append(h_last)
            lc_s.append(cb.transpose(1, 0, 2))
        sp = sp.with_x(ffn(sp, li, 2, *f2, npre, npost))
        ss = ss.with_x(ffn(ss, li, 2, *f2, npre, npost))

    return (sp.x, ss.x.reshape(Bd, T, D), jnp.stack(kv_p), jnp.stack(kv_s), jnp.stack(hg_p), jnp.stack(hg_s),
            jnp.stack(lh_p), jnp.stack(lh_s), jnp.stack(lc_p), jnp.stack(lc_s))
```

```python
import functools

import jax
import jax.numpy as jnp
from jax import lax
from jax.experimental import pallas as pl
from jax.experimental.pallas import tpu as pltpu

F32 = jnp.float32
BF16 = jnp.bfloat16

DEPTH = 4
N_MIXERS = 3
NORM_EPS = 1e-6
N_MOD = 9
FFN_RES = 0.5

MLA_HEADS = 16
Q_LORA = 512
KV_LORA = 512
QK_NOPE = 128
QK_ROPE = 64
V_HEAD = 128
KV_ROW = KV_LORA + QK_ROPE
KV_PAD = 640
QK_PAD = 256
MLA_SCALE = (QK_NOPE + QK_ROPE) ** -0.5
ROPE_THETA = 10000.0
PAGE_SIZE = 128
NEG_INF = -1e30

HG_HEADS = 16
HG_DK = 128
HG_DV = 128
HG_CHUNK = 16
HG_BLOCK = 128

LRU_BLOCKS = 16
LRU_BW = 128
CONV_W = 4
LRU_C = 8.0

LANE = 128
VMEM_LIMIT = 56 * 1024 * 1024


def _cparams(*sem):
    return pltpu.CompilerParams(dimension_semantics=sem, vmem_limit_bytes=VMEM_LIMIT)


def _dot(a, b):
    return jnp.dot(a, b, preferred_element_type=F32)


def _dot_nt(a, b):
    return lax.dot_general(a, b, (((1,), (1,)), ((), ())), preferred_element_type=F32)


def _rms(x, g):
    ms = jnp.mean(x * x, axis=-1, keepdims=True)
    return x * lax.rsqrt(ms + NORM_EPS) * g


def _sigmoid(x):
    return 1.0 / (1.0 + jnp.exp(-x))


def _silu(x):
    return x * _sigmoid(x)


def _softplus(x):
    return jnp.maximum(x, 0.0) + jnp.log1p(jnp.exp(-jnp.abs(x)))


def _gelu_tanh(x):
    c = 0.7978845608028654
    return 0.5 * x * (1.0 + jnp.tanh(c * (x + 0.044715 * (x * x * x))))


def _prenorm(x_ref, sh_ref, sc_ref, g_ref):
    return (_rms(x_ref[...], g_ref[...]) * (1.0 + sc_ref[...]) + sh_ref[...]).astype(BF16)


class Stream:
    def __init__(self, x, mod, per_row):
        self.x, self.mod, self.per_row = x, mod, per_row
        self.G, self.R, self.D = x.shape
        self.bm = min(512, self.R)

    def with_x(self, x):
        return Stream(x, self.mod, self.per_row)

    def rows(self, width):
        return pl.BlockSpec((None, self.bm, width), lambda g, i, *_: (g, i, 0))

    def mod_spec(self, li, k):
        G, per_row = self.G, self.per_row
        rm = self.bm if per_row else 1
        return pl.BlockSpec((None, rm, self.D), lambda g, i, *_: (li * G + g, i if per_row else 0, k))


def _gain_spec(idx, width):
    return pl.BlockSpec((None, 1, width), lambda *_: (idx, 0, 0))


def _ada_kernel(c_ref, w_ref, b_ref, o_ref):
    cs = _silu(c_ref[...]).astype(BF16)
    o_ref[...] = _dot(cs, w_ref[...].astype(BF16)) + b_ref[...]


def ada_mod(c_all, ada_w, ada_b):
    Bc, D = c_all.shape
    N = ada_w.shape[-1]
    bn = 1024
    return pl.pallas_call(
        _ada_kernel,
        grid=(DEPTH, N // bn),
        in_specs=[pl.BlockSpec((Bc, D), lambda l, j: (0, 0)),
                  pl.BlockSpec((None, D, bn), lambda l, j: (l, 0, j)),
                  pl.BlockSpec((None, 1, bn), lambda l, j: (l, 0, j))],
        out_specs=pl.BlockSpec((None, Bc, bn), lambda l, j: (l, 0, j)),
        out_shape=jax.ShapeDtypeStruct((DEPTH, Bc, N), F32),
        compiler_params=_cparams("parallel", "parallel"),
        name="ada_mod",
    )(c_all, ada_w, ada_b.reshape(DEPTH, 1, N))


def _ffn_kernel(x_ref, sh_ref, sc_ref, gt_ref, gpre_ref, gpost_ref, wg_ref, wu_ref, wd_ref, o_ref, *rest, emit):
    h_ref, acc_ref = rest[-2:]
    j = pl.program_id(2)

    @pl.when(j == 0)
    def _():
        h_ref[...] = _prenorm(x_ref, sh_ref, sc_ref, gpre_ref)
        acc_ref[...] = jnp.zeros_like(acc_ref)

    wg, wu, wd = wg_ref[...].astype(BF16), wu_ref[...].astype(BF16), wd_ref[...].astype(BF16)
    if emit:
        for out_ref, w in zip(rest[:3], (wg, wu, wd)):
            out_ref[...] = w
    h = h_ref[...]
    g = _dot(h, wg)
    u = _dot(h, wu)
    acc_ref[...] += _dot((_silu(g) * u).astype(BF16), wd)

    @pl.when(j == pl.num_programs(2) - 1)
    def _():
        o_ref[...] = x_ref[...] + FFN_RES * gt_ref[...] * _rms(acc_ref[...], gpost_ref[...])


def ffn(s, li, sub, wg, wu, wd, norm_pre, norm_post, emit):
    D, DF = wg.shape[-2:]
    k0 = 3 * sub
    if emit:
        assert s.G * (s.R // s.bm) == 1
        bf = 256
        w_in = [pl.BlockSpec((None, D, bf), lambda g, i, j: (li, 0, j)),
                pl.BlockSpec((None, D, bf), lambda g, i, j: (li, 0, j)),
                pl.BlockSpec((None, bf, D), lambda g, i, j: (li, j, 0))]
    else:
        bf = 512
        w_in = []
    w_2d = [pl.BlockSpec((D, bf), lambda g, i, j: (0, j)), pl.BlockSpec((D, bf), lambda g, i, j: (0, j)),
            pl.BlockSpec((bf, D), lambda g, i, j: (j, 0))]
    x_shape = jax.ShapeDtypeStruct(s.x.shape, F32)
    w_shapes = [jax.ShapeDtypeStruct((D, DF), BF16), jax.ShapeDtypeStruct((D, DF), BF16),
                jax.ShapeDtypeStruct((DF, D), BF16)]
    out = pl.pallas_call(
        functools.partial(_ffn_kernel, emit=emit),
        grid=(s.G, s.R // s.bm, DF // bf),
        in_specs=[s.rows(D), s.mod_spec(li, k0), s.mod_spec(li, k0 + 1), s.mod_spec(li, k0 + 2),
                  _gain_spec(li * 3 + sub, D), _gain_spec(li * 3 + sub, D)] + (w_in if emit else w_2d),
        out_specs=[s.rows(D)] + w_2d if emit else s.rows(D),
        out_shape=[x_shape] + w_shapes if emit else x_shape,
        scratch_shapes=[pltpu.VMEM((s.bm, D), BF16), pltpu.VMEM((s.bm, D), F32)],
        compiler_params=_cparams("parallel", "parallel", "arbitrary"),
        name="ffn_emit" if emit else "ffn",
    )(s.x, s.mod, s.mod, s.mod, norm_pre, norm_post, wg, wu, wd)
    return (out[0], out[1:]) if emit else out


def _mm_post_kernel(a_ref, w_ref, x_ref, gt_ref, gpost_ref, o_ref):
    y = _dot(a_ref[...], w_ref[...])
    o_ref[...] = x_ref[...] + gt_ref[...] * _rms(y, gpost_ref[...])


def mm_post(s, li, a, w, norm_post):
    K, D = w.shape
    bm = min(256, s.R)
    rows = lambda width: pl.BlockSpec((None, bm, width), lambda g, i: (g, i, 0))
    G, per_row = s.G, s.per_row
    gate = pl.BlockSpec((None, bm if per_row else 1, D), lambda g, i: (li * G + g, i if per_row else 0, 5))
    return pl.pallas_call(
        _mm_post_kernel,
        grid=(s.G, s.R // bm),
        in_specs=[rows(K), pl.BlockSpec((K, D), lambda g, i: (0, 0)), rows(D), gate, _gain_spec(li * 3 + 1, D)],
        out_specs=rows(D),
        out_shape=jax.ShapeDtypeStruct(s.x.shape, F32),
        compiler_params=_cparams("parallel", "parallel"),
        name="mm_post",
    )(a, w, s.x, s.mod, norm_post)


def _mla_proj_kernel(x_ref, sh_ref, sc_ref, gpre_ref, w_ref, gq_ref, gkv_ref, cs_ref, ql_ref, kv_ref, kvb_ref):
    h = _prenorm(x_ref, sh_ref, sc_ref, gpre_ref)
    r = _dot(h, w_ref[...])
    ql_ref[...] = _rms(r[:, :Q_LORA], gq_ref[...]).astype(BF16)
    ckv = _rms(r[:, Q_LORA:Q_LORA + KV_LORA], gkv_ref[...])
    t = r[:, Q_LORA + KV_LORA:] * cs_ref[...]
    kr = t[:, :QK_ROPE] + t[:, QK_ROPE:]
    kv_ref[:, :KV_LORA] = ckv
    kv_ref[:, KV_LORA:] = kr
    kvb_ref[:, :KV_LORA] = ckv.astype(BF16)
    lane = lax.broadcasted_iota(jnp.int32, (kr.shape[0], KV_PAD - KV_LORA), 1)
    tail = jnp.where(lane == QK_ROPE, 1.0, jnp.concatenate([kr, jnp.zeros_like(kr)], axis=-1))
    kvb_ref[:, KV_LORA:] = tail.astype(BF16)


def mla_proj(s, li, w_cat, g_q, g_kv, cs_tab, norm_pre):
    D, N = w_cat.shape
    out = (jax.ShapeDtypeStruct((s.G, s.R, Q_LORA), BF16),
           jax.ShapeDtypeStruct((s.G, s.R, KV_ROW), F32),
           jax.ShapeDtypeStruct((s.G, s.R, KV_PAD), BF16))
    return pl.pallas_call(
        _mla_proj_kernel,
        grid=(s.G, s.R // s.bm),
        in_specs=[s.rows(D), s.mod_spec(li, 3), s.mod_spec(li, 4), _gain_spec(li * 3 + 1, D),
                  pl.BlockSpec((D, N), lambda g, i: (0, 0)),
                  pl.BlockSpec((1, Q_LORA), lambda g, i: (0, 0)),
                  pl.BlockSpec((1, KV_LORA), lambda g, i: (0, 0)),
                  pl.BlockSpec((s.bm, LANE), lambda g, i: (i, 0))],
        out_specs=(s.rows(Q_LORA), s.rows(KV_ROW), s.rows(KV_PAD)),
        out_shape=out,
        compiler_params=_cparams("parallel", "parallel"),
        name="mla_proj",
    )(s.x, s.mod, s.mod, norm_pre, w_cat, g_q, g_kv, cs_tab)


def _q_expand_kernel(ql_ref, w_ref, wr_ref, cm_ref, sm_ref, q_ref, *, hb):
    ql = ql_ref[...]
    a = _dot(ql, w_ref[...])
    b = _dot(ql, wr_ref[...])
    cm, sm = cm_ref[...], sm_ref[...]
    for h in range(hb):
        sl = slice(h * QK_PAD, (h + 1) * QK_PAD)
        q_ref[:, sl] = (a[:, sl] * cm + b[:, sl] * sm).astype(BF16)


def q_expand(s, ql, w_q, w_qrot, cmap, smap):
    hb = 4
    nw = hb * QK_PAD
    wspec = pl.BlockSpec((Q_LORA, nw), lambda g, i, j: (0, j))
    tspec = pl.BlockSpec((s.bm, QK_PAD), lambda g, i, j: (i, 0))
    return pl.pallas_call(
        functools.partial(_q_expand_kernel, hb=hb),
        grid=(s.G, s.R // s.bm, MLA_HEADS // hb),
        in_specs=[s.rows(Q_LORA), wspec, wspec, tspec, tspec],
        out_specs=pl.BlockSpec((None, s.bm, nw), lambda g, i, j: (g, i, j)),
        out_shape=jax.ShapeDtypeStruct((s.G, s.R, MLA_HEADS * QK_PAD), BF16),
        compiler_params=_cparams("parallel", "parallel", "parallel"),
        name="q_expand",
    )(ql, w_q, w_qrot, cmap, smap)


def _kv_expand_kernel(kvb_ref, wk_ref, wv_ref, k_ref, v_ref):
    kvb = kvb_ref[...]
    k_ref[...] = _dot(kvb, wk_ref[...]).astype(BF16)
    v_ref[...] = _dot(kvb, wv_ref[...]).astype(BF16)


def kv_expand(s, kvb, w_k, w_v):
    hb = 4
    return pl.pallas_call(
        _kv_expand_kernel,
        grid=(s.G, s.R // s.bm, MLA_HEADS // hb),
        in_specs=[s.rows(KV_PAD),
                  pl.BlockSpec((KV_PAD, hb * QK_PAD), lambda g, i, j: (0, j)),
                  pl.BlockSpec((KV_PAD, hb * QK_PAD), lambda g, i, j: (0, j))],
        out_specs=(pl.BlockSpec((None, s.bm, hb * QK_PAD), lambda g, i, j: (g, i, j)),
                   pl.BlockSpec((None, s.bm, hb * QK_PAD), lambda g, i, j: (g, i, j))),
        out_shape=(jax.ShapeDtypeStruct((s.G, s.R, MLA_HEADS * QK_PAD), BF16),
                   jax.ShapeDtypeStruct((s.G, s.R, MLA_HEADS * QK_PAD), BF16)),
        compiler_params=_cparams("parallel", "parallel", "parallel"),
        name="kv_expand",
    )(kvb, w_k, w_v)


def _flash_kernel(q_ref, k_ref, v_ref, o_ref, m_ref, acc_ref, *, blk, hb):
    i = pl.program_id(2)
    m_ref[...] = jnp.full_like(m_ref, NEG_INF)
    acc_ref[...] = jnp.zeros_like(acc_ref)

    def block(j, diagonal):
        r0 = pl.multiple_of(j * blk, blk)
        for h in range(hb):
            hs = slice(h * QK_PAD, (h + 1) * QK_PAD)
            s = _dot_nt(q_ref[:, hs], k_ref[pl.ds(r0, blk), hs])
            if diagonal:
                row = lax.broadcasted_iota(jnp.int32, (blk, blk), 0)
                col = lax.broadcasted_iota(jnp.int32, (blk, blk), 1)
                s = jnp.where(col <= row, s, NEG_INF)
            m_prev = m_ref[h]
            m_new = jnp.maximum(m_prev, jnp.max(s, axis=-1, keepdims=True))
            alpha = jnp.exp(m_prev - m_new)
            p = jnp.exp(s - jnp.concatenate([m_new] * (blk // LANE), axis=1))
            pv = _dot(p.astype(BF16), v_ref[pl.ds(r0, blk), hs])
            acc_ref[h] = jnp.concatenate([alpha] * (QK_PAD // LANE), axis=1) * acc_ref[h] + pv
            m_ref[h] = m_new

    def below_diagonal(j, carry):
        block(j, False)
        return carry

    lax.fori_loop(0, i, below_diagonal, 0)
    block(i, True)
    for h in range(hb):
        acc = acc_ref[h]
        o_ref[:, h * V_HEAD:(h + 1) * V_HEAD] = (acc[:, :V_HEAD] / acc[:, V_HEAD:V_HEAD + 1]).astype(o_ref.dtype)


def flash_prompt(q, k, v):
    B, L, _ = q.shape
    blk = min(512, L)
    hb = 4
    seq = pl.BlockSpec((None, L, hb * QK_PAD), lambda b, h, i: (b, 0, h))
    return pl.pallas_call(
        functools.partial(_flash_kernel, blk=blk, hb=hb),
        grid=(B, MLA_HEADS // hb, L // blk),
        in_specs=[pl.BlockSpec((None, blk, hb * QK_PAD), lambda b, h, i: (b, i, h)), seq, seq],
        out_specs=pl.BlockSpec((None, blk, hb * V_HEAD), lambda b, h, i: (b, i, h)),
        out_shape=jax.ShapeDtypeStruct((B, L, MLA_HEADS * V_HEAD), BF16),
        scratch_shapes=[pltpu.VMEM((hb, blk, LANE), F32), pltpu.VMEM((hb, blk, QK_PAD), F32)],
        compiler_params=_cparams("parallel", "parallel", "parallel"),
        name="flash_prompt",
    )(q, k, v)


def _q_absorb_kernel(q_ref, w_ref, qc_ref):
    q = q_ref[...]
    qc_ref[:, :KV_LORA] = _dot(q[:, :QK_NOPE], w_ref[...]).astype(BF16)
    qc_ref[:, KV_LORA:] = q[:, QK_NOPE:]


def q_absorb(q, w_ukT):
    R = q.shape[0]
    return pl.pallas_call(
        _q_absorb_kernel,
        grid=(MLA_HEADS,),
        in_specs=[pl.BlockSpec((R, QK_PAD), lambda h: (0, h)),
                  pl.BlockSpec((None, QK_NOPE, KV_LORA), lambda h: (h, 0, 0))],
        out_specs=pl.BlockSpec((None, R, KV_PAD), lambda h: (h, 0, 0)),
        out_shape=jax.ShapeDtypeStruct((MLA_HEADS, R, KV_PAD), BF16),
        compiler_params=_cparams("parallel"),
        name="q_absorb",
    )(q, w_ukT)


def _paged_kernel(pt_ref, qc_ref, newt_ref, *rest, pp, group, heads):
    pages = rest[:pp]
    o_ref, kb_ref, kn_ref, m_ref, l_ref, acc_ref = rest[pp:]
    j = pl.program_id(1)
    rows = qc_ref.shape[0]
    gw = group * PAGE_SIZE

    @pl.when(j == 0)
    def _():
        m_ref[...] = jnp.full_like(m_ref, NEG_INF)
        l_ref[...] = jnp.zeros_like(l_ref)
        acc_ref[...] = jnp.zeros_like(acc_ref)
        kb_ref[KV_ROW:, :] = jnp.zeros((KV_PAD - KV_ROW, kb_ref.shape[1]), BF16)

    def update(key_refs, visible):
        q = qc_ref[...]
        ss = [_dot(q, kr[...]) for kr in key_refs]
        if visible is not None:
            ss = [jnp.where(visible, s, NEG_INF) for s in ss]
        m_prev = m_ref[...]
        m_new = m_prev
        for s in ss:
            m_new = jnp.maximum(m_new, jnp.max(s, axis=-1, keepdims=True))
        alpha = jnp.exp(m_prev - m_new)
        l_new = alpha * l_ref[...]
        acc = alpha * acc_ref[...]
        for s, kr in zip(ss, key_refs):
            p = jnp.exp(s - m_new)
            l_new = l_new + jnp.sum(p, axis=-1, keepdims=True)
            acc = acc + _dot_nt(p.astype(BF16), kr[:KV_LORA, :])
        l_ref[...] = l_new
        acc_ref[...] = acc
        m_ref[...] = m_new

    for k in range(pp):
        kb_ref[:KV_ROW, k * PAGE_SIZE:(k + 1) * PAGE_SIZE] = pages[k][...].astype(BF16)
    update([kb_ref.at[:, g * gw:(g + 1) * gw] for g in range(pp // group)], None)

    @pl.when(j == pl.num_programs(1) - 1)
    def _():
        T = newt_ref.shape[1]
        kn_ref[...] = jnp.zeros_like(kn_ref)
        kn_ref[:KV_ROW, :T] = newt_ref[...].astype(BF16)
        row = lax.broadcasted_iota(jnp.int32, (rows, PAGE_SIZE), 0)
        col = lax.broadcasted_iota(jnp.int32, (rows, PAGE_SIZE), 1)
        update([kn_ref], col * heads <= row)
        o_ref[...] = acc_ref[...] / l_ref[...]


def paged_attention(qc, kv_new, cache_t, layer, page_table):
    Bd, rows, _ = qc.shape
    T = kv_new.shape[1]
    n_pages = page_table.shape[1]
    pp = 32 if n_pages % 32 == 0 else n_pages
    group = 8 if pp % 8 == 0 else pp
    page_specs = [pl.BlockSpec((None, None, KV_ROW, PAGE_SIZE),
                               functools.partial(lambda b, j, pt, k: (layer, pt[b, j * pp + k], 0, 0), k=k))
                  for k in range(pp)]
    grid_spec = pltpu.PrefetchScalarGridSpec(
        num_scalar_prefetch=1,
        grid=(Bd, n_pages // pp),
        in_specs=[pl.BlockSpec((None, rows, KV_PAD), lambda b, j, pt: (b, 0, 0)),
                  pl.BlockSpec((None, KV_ROW, T), lambda b, j, pt: (b, 0, 0))] + page_specs,
        out_specs=pl.BlockSpec((None, rows, KV_LORA), lambda b, j, pt: (b, 0, 0)),
        scratch_shapes=[pltpu.VMEM((KV_PAD, pp * PAGE_SIZE), BF16), pltpu.VMEM((KV_PAD, PAGE_SIZE), BF16),
                        pltpu.VMEM((rows, 1), F32), pltpu.VMEM((rows, 1), F32), pltpu.VMEM((rows, KV_LORA), F32)])
    return pl.pallas_call(
        functools.partial(_paged_kernel, pp=pp, group=group, heads=rows // T),
        grid_spec=grid_spec,
        out_shape=jax.ShapeDtypeStruct((Bd, rows, KV_LORA), F32),
        compiler_params=_cparams("parallel", "arbitrary"),
        name="paged_attention",
    )(page_table, qc, kv_new.transpose(0, 2, 1), *([cache_t] * pp))


def _v_up_kernel(o_ref, w_ref, y_ref):
    y_ref[...] = _dot(o_ref[...].astype(BF16), w_ref[...]).astype(BF16)


def v_up(o_lat, w_uv):
    R = o_lat.shape[0]
    return pl.pallas_call(
        _v_up_kernel,
        grid=(MLA_HEADS,),
        in_specs=[pl.BlockSpec((R, KV_LORA), lambda h: (0, h)), pl.BlockSpec((KV_LORA, V_HEAD), lambda h: (0, h))],
        out_specs=pl.BlockSpec((R, V_HEAD), lambda h: (0, h)),
        out_shape=jax.ShapeDtypeStruct((R, MLA_HEADS * V_HEAD), BF16),
        compiler_params=_cparams("parallel"),
        name="v_up",
    )(o_lat, w_uv)


def _hg_proj_kernel(x_ref, sh_ref, sc_ref, gpre_ref, wq_ref, wf_ref, wi_ref, wg_ref, lb_ref,
                    q_ref, k_ref, v_ref, lf_ref, g_ref, h_ref):
    @pl.when(pl.program_id(2) == 0)
    def _():
        h_ref[...] = _prenorm(x_ref, sh_ref, sc_ref, gpre_ref)

    h = h_ref[...]
    q_ref[...] = _silu(_dot(h, wq_ref[...]))
    fz = _dot(h, wf_ref[...])
    lb = lb_ref[...]
    a = jnp.log(lb)
    b = jnp.log1p(-lb) + (jnp.minimum(fz, 0.0) - jnp.log1p(jnp.exp(-jnp.abs(fz))))
    lf_ref[...] = jnp.maximum(a, b) + jnp.log1p(jnp.exp(-jnp.abs(a - b)))
    k_ref[...] = (1.0 - lb) * _sigmoid(-fz)
    v_ref[...] = _dot(h, wi_ref[...])
    g_ref[...] = _silu(_dot(h, wg_ref[...]))


def hg_proj(s, li, w_q, w_f, w_i, w_g, lb, norm_pre):
    D, N = w_q.shape
    bn = 512
    wspec = pl.BlockSpec((D, bn), lambda g, i, j: (0, j))
    ospec = pl.BlockSpec((None, s.bm, bn), lambda g, i, j: (g, i, j))
    oshape = jax.ShapeDtypeStruct((s.G, s.R, N), F32)
    return pl.pallas_call(
        _hg_proj_kernel,
        grid=(s.G, s.R // s.bm, N // bn),
        in_specs=[s.rows(D), s.mod_spec(li, 3), s.mod_spec(li, 4), _gain_spec(li * 3 + 1, D),
                  wspec, wspec, wspec, wspec, pl.BlockSpec((1, bn), lambda g, i, j: (0, j))],
        out_specs=(ospec,) * 5,
        out_shape=(oshape,) * 5,
        scratch_shapes=[pltpu.VMEM((s.bm, D), BF16)],
        compiler_params=_cparams("parallel", "parallel", "arbitrary"),
        name="hg_proj",
    )(s.x, s.mod, s.mod, norm_pre, w_q, w_f, w_i, w_g, lb)


def _cumsum_rows(x, span):
    row = lax.broadcasted_iota(jnp.int32, x.shape, 0)
    pos = jnp.bitwise_and(row, span - 1)
    sh = 1
    while sh < span:
        x = x + jnp.where(pos >= sh, pltpu.roll(x, sh, 0), 0.0)
        sh *= 2
    return x


def _chunk_id(r):
    return lax.shift_right_logical(r, HG_CHUNK.bit_length() - 1)


def _hg_scan_kernel(q_ref, k_ref, v_ref, lf_ref, g_ref, s0_ref, gn_ref, o_ref, so_ref, st_ref, *, hb, chained):
    n = HG_BLOCK // HG_CHUNK
    first = pl.program_id(2) == 0
    last = pl.program_id(2) == pl.num_programs(2) - 1

    if chained:
        @pl.when(first)
        def _():
            for h in range(hb):
                st_ref[h] = s0_ref[h].T

    row = lax.broadcasted_iota(jnp.int32, (HG_BLOCK, HG_BLOCK), 0)
    col = lax.broadcasted_iota(jnp.int32, (HG_BLOCK, HG_BLOCK), 1)
    causal = (_chunk_id(row) == _chunk_id(col)) & (col <= row)
    chunk_of_row = _chunk_id(lax.broadcasted_iota(jnp.int32, (HG_BLOCK, HG_DK), 0))

    for h in range(hb):
        sl = slice(h * HG_DK, (h + 1) * HG_DK)
        b = _cumsum_rows(lf_ref[:, sl], HG_CHUNK)
        b_last = jnp.concatenate(
            [jnp.broadcast_to(b[c * HG_CHUNK + HG_CHUNK - 1:(c + 1) * HG_CHUNK, :], (HG_CHUNK, HG_DK)) for c in range(n)],
            axis=0)
        q, k, v = q_ref[:, sl], k_ref[:, sl], v_ref[:, sl]
        qg = q * jnp.exp(b)
        kd = (k * jnp.exp(-b)).astype(BF16)
        kl = k * jnp.exp(b_last - b)
        qgb = qg.astype(BF16)
        vb = v.astype(BF16)
        a = jnp.where(causal, _dot_nt(qgb, kd), 0.0)
        o = _dot(a.astype(BF16), vb)
        klx = jnp.concatenate([jnp.where(chunk_of_row == c, kl, 0.0).astype(BF16) for c in range(n)], axis=1)
        ut = _dot(v.T.astype(BF16), klx)
        sts = []
        if chained:
            st = st_ref[h]
        for c in range(n):
            if not chained:
                st = s0_ref[c, h].T
            sts.append(st.astype(BF16))
            dec = jnp.exp(b_last[c * HG_CHUNK:c * HG_CHUNK + 1, :])
            st = st * dec + ut[:, c * HG_DK:(c + 1) * HG_DK]
            if not chained:
                so_ref[c, h] = st.T
        if chained:
            st_ref[h] = st

            @pl.when(last)
            def _():
                so_ref[h] = st.T
        qx = jnp.concatenate([jnp.where(chunk_of_row == c, qg, 0.0).astype(BF16) for c in range(n)], axis=1)
        o = o + _dot_nt(qx, jnp.concatenate(sts, axis=1))
        o_ref[:, sl] = (_rms(o, gn_ref[...]) * g_ref[:, sl]).astype(BF16)


def hg_scan(q, k, v, lf, gate, s0, g_norm, chained):
    G, R, N = q.shape
    hb = 4 if chained else 2
    n = HG_BLOCK // HG_CHUNK
    rows = pl.BlockSpec((None, HG_BLOCK, hb * HG_DK), lambda g, h, l: (g, l, h))
    if chained:
        sspec = pl.BlockSpec((None, hb, HG_DK, HG_DV), lambda g, h, l: (g, h, 0, 0))
    else:
        sspec = pl.BlockSpec((None, n, hb, HG_DK, HG_DV), lambda g, h, l: (g, 0, h, 0, 0))
    return pl.pallas_call(
        functools.partial(_hg_scan_kernel, hb=hb, chained=chained),
        grid=(G, HG_HEADS // hb, R // HG_BLOCK),
        in_specs=[rows, rows, rows, rows, rows, sspec, pl.BlockSpec((1, HG_DV), lambda g, h, l: (0, 0))],
        out_specs=(rows, sspec),
        out_shape=(jax.ShapeDtypeStruct((G, R, N), BF16), jax.ShapeDtypeStruct(s0.shape, F32)),
        scratch_shapes=[pltpu.VMEM((hb, HG_DV, HG_DK), F32)],
        compiler_params=_cparams("parallel", "parallel", "arbitrary"),
        name="hg_scan",
    )(q, k, v, lf, gate, s0, g_norm)


def _lru_proj_kernel(x_ref, sh_ref, sc_ref, gpre_ref, wx_ref, wy_ref, u_ref, g_ref, h_ref):
    @pl.when(pl.program_id(2) == 0)
    def _():
        h_ref[...] = _prenorm(x_ref, sh_ref, sc_ref, gpre_ref)

    h = h_ref[...]
    u_ref[...] = _dot(h, wx_ref[...])
    g_ref[...] = _gelu_tanh(_dot(h, wy_ref[...]))


def lru_proj(s, li, w_x, w_y, norm_pre):
    D, N = w_x.shape
    bn = 512
    wspec = pl.BlockSpec((D, bn), lambda g, i, j: (0, j))
    ospec = pl.BlockSpec((None, s.bm, bn), lambda g, i, j: (g, i, j))
    oshape = jax.ShapeDtypeStruct((s.G, s.R, N), F32)
    return pl.pallas_call(
        _lru_proj_kernel,
        grid=(s.G, s.R // s.bm, N // bn),
        in_specs=[s.rows(D), s.mod_spec(li, 3), s.mod_spec(li, 4), _gain_spec(li * 3 + 1, D), wspec, wspec],
        out_specs=(ospec, ospec),
        out_shape=(oshape, oshape),
        scratch_shapes=[pltpu.VMEM((s.bm, D), BF16)],
        compiler_params=_cparams("parallel", "parallel", "arbitrary"),
        name="lru_proj",
    )(s.x, s.mod, s.mod, norm_pre, w_x, w_y)


def _lru_gates(xc, n, wra_ref, bra_ref, wix_ref, bix_ref, lam_ref):
    sl = slice(n * LRU_BW, (n + 1) * LRU_BW)
    xb = xc.astype(BF16)
    r = _sigmoid(_dot(xb, wra_ref[n]) + bra_ref[:, sl])
    ig = _sigmoid(_dot(xb, wix_ref[n]) + bix_ref[:, sl])
    log_a = -LRU_C * r * _softplus(-lam_ref[:, sl])
    a = jnp.exp(log_a)
    return a, jnp.sqrt(-jnp.tanh(log_a) * (a * a + 1.0)) * (ig * xc)


def _lru_scan_kernel(u_ref, gate_ref, cb_ref, h0_ref, cw_ref, cbias_ref, wra_ref, bra_ref, wix_ref, bix_ref, lam_ref,
                     yg_ref, hl_ref, cbo_ref, ext_ref, a_ref, b_ref, hc_ref, *, bl):
    l = pl.program_id(1)
    keep = CONV_W - 1
    base = 8

    @pl.when(l == 0)
    def _():
        ext_ref[base - keep:base, :] = cb_ref[...]
        hc_ref[...] = h0_ref[...]

    ext_ref[base:base + bl, :] = u_ref[...]
    for n in range(LRU_BLOCKS):
        sl = slice(n * LRU_BW, (n + 1) * LRU_BW)
        xc = cbias_ref[:, sl]
        for j in range(CONV_W):
            xc = xc + ext_ref[base - keep + j:base - keep + j + bl, sl] * cw_ref[j:j + 1, sl]
        a, b = _lru_gates(xc, n, wra_ref, bra_ref, wix_ref, bix_ref, lam_ref)
        a_ref[:, sl] = a
        b_ref[:, sl] = b

    pos = lax.broadcasted_iota(jnp.int32, (8, a_ref.shape[1]), 0)

    def tile(t, h_prev):
        r0 = pl.multiple_of(t * 8, 8)
        a_t = a_ref[pl.ds(r0, 8), :]
        b_t = b_ref[pl.ds(r0, 8), :]
        for sh in (1, 2, 4):
            m = pos >= sh
            b_t = jnp.where(m, a_t * pltpu.roll(b_t, sh, 0) + b_t, b_t)
            a_t = jnp.where(m, a_t * pltpu.roll(a_t, sh, 0), a_t)
        h_t = b_t + a_t * h_prev
        b_ref[pl.ds(r0, 8), :] = h_t
        return h_t[7:8, :]

    h_last = lax.fori_loop(0, bl // 8, tile, hc_ref[...])
    hc_ref[...] = h_last
    yg_ref[...] = (b_ref[...] * gate_ref[...]).astype(BF16)
    tail = ext_ref[base + bl - keep:base + bl, :]
    ext_ref[base - keep:base, :] = tail

    @pl.when(l == pl.num_programs(1) - 1)
    def _():
        hl_ref[...] = h_last
        cbo_ref[...] = tail


def lru_scan(u, gate, conv_buf, h0, conv_w, conv_b, w_ra, b_ra, w_ix, b_ix, lam):
    B, L, W = u.shape
    bl = min(256, L)
    rows = pl.BlockSpec((None, bl, W), lambda b, l: (b, l, 0))
    vec = pl.BlockSpec((1, W), lambda b, l: (0, 0))
    wblk = pl.BlockSpec((LRU_BLOCKS, LRU_BW, LRU_BW), lambda b, l: (0, 0, 0))
    hspec = pl.BlockSpec((None, 1, W), lambda b, l: (b, 0, 0))
    cspec = pl.BlockSpec((None, CONV_W - 1, W), lambda b, l: (b, 0, 0))
    return pl.pallas_call(
        functools.partial(_lru_scan_kernel, bl=bl),
        grid=(B, L // bl),
        in_specs=[rows, rows, cspec, hspec, pl.BlockSpec((CONV_W, W), lambda b, l: (0, 0)), vec,
                  wblk, vec, wblk, vec, vec],
        out_specs=(rows, hspec, cspec),
        out_shape=(jax.ShapeDtypeStruct((B, L, W), BF16), jax.ShapeDtypeStruct((B, 1, W), F32),
                   jax.ShapeDtypeStruct((B, CONV_W - 1, W), F32)),
        scratch_shapes=[pltpu.VMEM((bl + 8, W), F32), pltpu.VMEM((bl, W), F32), pltpu.VMEM((bl, W), F32),
                        pltpu.VMEM((1, W), F32)],
        compiler_params=_cparams("parallel", "arbitrary"),
        name="lru_scan",
    )(u, gate, conv_buf, h0.reshape(B, 1, W), conv_w, conv_b, w_ra, b_ra, w_ix, b_ix, lam)


def _lru_step_kernel(u_ref, gate_ref, cb_ref, h0_ref, cw_ref, cbias_ref, wra_ref, bra_ref, wix_ref, bix_ref, lam_ref,
                     yg_ref, hl_ref, cbo_ref):
    T = u_ref.shape[0]
    keep = CONV_W - 1
    for n in range(LRU_BLOCKS):
        sl = slice(n * LRU_BW, (n + 1) * LRU_BW)
        ext = [cb_ref[j, :, sl] for j in range(keep)] + [u_ref[t, :, sl] for t in range(T)]
        h = h0_ref[:, sl]
        for t in range(T):
            xc = cbias_ref[:, sl]
            for j in range(CONV_W):
                xc = xc + ext[t + j] * cw_ref[j:j + 1, sl]
            a, b = _lru_gates(xc, n, wra_ref, bra_ref, wix_ref, bix_ref, lam_ref)
            h = a * h + b
            yg_ref[t, :, sl] = (h * gate_ref[t, :, sl]).astype(BF16)
        hl_ref[:, sl] = h
        for j in range(keep):
            cbo_ref[j, :, sl] = ext[T + j]


def lru_step(u, gate, conv_buf, h0, conv_w, conv_b, w_ra, b_ra, w_ix, b_ix, lam):
    T, Bd, W = u.shape
    full = lambda *shape: pl.BlockSpec(shape, lambda i: (0,) * len(shape))
    return pl.pallas_call(
        _lru_step_kernel,
        grid=(1,),
        in_specs=[full(T, Bd, W), full(T, Bd, W), full(CONV_W - 1, Bd, W), full(Bd, W), full(CONV_W, W), full(1, W),
                  full(LRU_BLOCKS, LRU_BW, LRU_BW), full(1, W), full(LRU_BLOCKS, LRU_BW, LRU_BW), full(1, W), full(1, W)],
        out_specs=(full(T, Bd, W), full(Bd, W), full(CONV_W - 1, Bd, W)),
        out_shape=(jax.ShapeDtypeStruct((T, Bd, W), BF16), jax.ShapeDtypeStruct((Bd, W), F32),
                   jax.ShapeDtypeStruct((CONV_W - 1, Bd, W), F32)),
        compiler_params=_cparams("arbitrary"),
        name="lru_step",
    )(u, gate, conv_buf, h0, conv_w, conv_b, w_ra, b_ra, w_ix, b_ix, lam)


def _rope_tables(pos):
    half = QK_ROPE // 2
    inv = ROPE_THETA ** (-jnp.arange(half, dtype=F32) / half)
    ang = pos.astype(F32)[:, None] * inv[None, :]
    cos, sin = jnp.cos(ang), jnp.sin(ang)
    cos2, sin2 = jnp.concatenate([cos, cos], -1), jnp.concatenate([sin, sin], -1)
    ones, zeros = jnp.ones((pos.shape[0], QK_NOPE), F32), jnp.zeros((pos.shape[0], QK_NOPE), F32)
    pad = jnp.zeros((pos.shape[0], QK_PAD - QK_NOPE - QK_ROPE), F32)
    cs_tab = jnp.concatenate([cos2, sin2], -1)
    cmap = jnp.concatenate([ones, cos2, pad], -1) * MLA_SCALE
    smap = jnp.concatenate([zeros, sin2, pad], -1) * MLA_SCALE
    return cs_tab, cmap, smap


def _rot_cols(w):
    half = w.shape[-1] // 2
    return jnp.concatenate([-w[..., half:], w[..., :half]], -1)


def _mla_weights(w_dq, w_uq, w_dkv, w_uk, w_uv):
    kr = w_dkv[:, KV_LORA:]
    w_cat = jnp.concatenate([w_dq, w_dkv, _rot_cols(kr)], axis=1).astype(BF16)
    uq = w_uq.reshape(Q_LORA, MLA_HEADS, QK_NOPE + QK_ROPE)
    nope, rope = uq[..., :QK_NOPE], uq[..., QK_NOPE:]
    pad = jnp.zeros((Q_LORA, MLA_HEADS, QK_PAD - QK_NOPE - QK_ROPE), F32)
    w_q = jnp.concatenate([nope, rope, pad], -1).reshape(Q_LORA, MLA_HEADS * QK_PAD).astype(BF16)
    w_qrot = jnp.concatenate([jnp.zeros_like(nope), _rot_cols(rope), pad], -1)
    w_qrot = w_qrot.reshape(Q_LORA, MLA_HEADS * QK_PAD).astype(BF16)
    w_k = jnp.zeros((KV_PAD, MLA_HEADS, QK_PAD), F32)
    w_k = w_k.at[:KV_LORA, :, :QK_NOPE].set(w_uk)
    eye = jnp.broadcast_to(jnp.eye(QK_ROPE, dtype=F32)[:, None, :], (QK_ROPE, MLA_HEADS, QK_ROPE))
    w_k = w_k.at[KV_LORA:KV_ROW, :, QK_NOPE:QK_NOPE + QK_ROPE].set(eye)
    w_k = w_k.reshape(KV_PAD, MLA_HEADS * QK_PAD).astype(BF16)
    w_v = jnp.zeros((KV_PAD, MLA_HEADS, QK_PAD), F32)
    w_v = w_v.at[:KV_LORA, :, :V_HEAD].set(w_uv).at[KV_ROW, :, V_HEAD].set(1.0)
    w_v = w_v.reshape(KV_PAD, MLA_HEADS * QK_PAD).astype(BF16)
    w_ukT = w_uk.transpose(1, 2, 0).astype(BF16)
    w_uv2 = w_uv.reshape(KV_LORA, MLA_HEADS * V_HEAD).astype(BF16)
    return w_cat, w_q, w_qrot, w_k, w_v, w_ukT, w_uv2


def _hg_lower_bounds(lb_param):
    cs = jnp.cumsum(jax.nn.softmax(lb_param.astype(F32), axis=0), axis=0)
    return cs - cs[0]


def kernel(x_prompt, x_sample, c_prompt, c_sample, cache_mla_kv, page_table, state_hgrn, state_lru_h, state_lru_conv, ada_w, ada_b, norm_pre, norm_post, ffn1_wg, ffn1_wu, ffn1_wd, ffn2_wg, ffn2_wu, ffn2_wd, mla_w_dq, mla_g_q, mla_w_uq, mla_w_dkv, mla_g_kv, mla_w_uk, mla_w_uv, mla_w_o, hg_lb, hg_w_q, hg_w_f, hg_w_i, hg_w_g, hg_g_norm, hg_w_o, lru_w_x, lru_w_y, lru_conv_w, lru_conv_b, lru_w_ra, lru_b_ra, lru_w_ix, lru_b_ix, lru_lam, lru_w_o):
    B, L, D = x_prompt.shape
    Bd, T, _ = x_sample.shape
    n_pages = page_table.shape[1]
    past_len = n_pages * PAGE_SIZE

    n_s = Bd * T
    c_all = jnp.concatenate([jnp.repeat(c_sample, T, axis=0), c_prompt, jnp.zeros((-(n_s + B) % 8, D), F32)], axis=0)
    mod = ada_mod(c_all, ada_w, ada_b)
    mod_p = mod[:, n_s:n_s + B].reshape(DEPTH * B, 1, N_MOD * D)
    sp = Stream(x_prompt, mod_p, per_row=False)
    ss = Stream(x_sample.reshape(1, n_s, D), mod, per_row=True)

    npre = norm_pre.reshape(DEPTH * 3, 1, D)
    npost = norm_post.reshape(DEPTH * 3, 1, D)
    bf = lambda w: w.astype(BF16)
    f1 = (ffn1_wg, ffn1_wu, ffn1_wd)
    f2 = (ffn2_wg, ffn2_wu, ffn2_wd)
    cache_t = jnp.swapaxes(cache_mla_kv, 2, 3)

    tab_p = _rope_tables(jnp.arange(L))
    tab_s = _rope_tables(jnp.tile(past_len + jnp.arange(T), Bd))
    lbs = _hg_lower_bounds(hg_lb)

    kv_p, kv_s, hg_p, hg_s, lh_p, lh_s, lc_p, lc_s = [], [], [], [], [], [], [], []
    for li in range(DEPTH):
        xs, w_bf = ffn(ss, li, 0, *f1, npre, npost, emit=True)
        ss = ss.with_x(xs)
        sp = sp.with_x(ffn(sp, li, 0, *w_bf, npre, npost, emit=False))
        kind, j = li % N_MIXERS, li // N_MIXERS
        if kind == 0:
            w_cat, w_q, w_qrot, w_k, w_v, w_ukT, w_uv2 = _mla_weights(
                mla_w_dq[j], mla_w_uq[j], mla_w_dkv[j], mla_w_uk[j], mla_w_uv[j])
            g_q, g_kv = mla_g_q[j].reshape(1, Q_LORA), mla_g_kv[j].reshape(1, KV_LORA)
            w_o = bf(mla_w_o[j])
            ql, kvr, kvb = mla_proj(sp, li, w_cat, g_q, g_kv, tab_p[0], npre)
            q = q_expand(sp, ql, w_q, w_qrot, tab_p[1], tab_p[2])
            kk, vv = kv_expand(sp, kvb, w_k, w_v)
            o = flash_prompt(q, kk, vv)
            sp = sp.with_x(mm_post(sp, li, o, w_o, npost))
            kv_p.append(kvr)
            ql, kvr, _ = mla_proj(ss, li, w_cat, g_q, g_kv, tab_s[0], npre)
            q = q_expand(ss, ql, w_q, w_qrot, tab_s[1], tab_s[2])
            qc = q_absorb(q[0], w_ukT)
            qc = qc.reshape(MLA_HEADS, Bd, T, KV_PAD).transpose(1, 2, 0, 3).reshape(Bd, T * MLA_HEADS, KV_PAD)
            kv_new = kvr.reshape(Bd, T, KV_ROW)
            o_lat = paged_attention(qc, kv_new, cache_t, j, page_table)
            o = v_up(o_lat.reshape(Bd * T, MLA_HEADS * KV_LORA), w_uv2)
            ss = ss.with_x(mm_post(ss, li, o.reshape(1, Bd * T, -1), w_o, npost))
            kv_s.append(kv_new)
        elif kind == 1:
            lb = lbs[li].reshape(1, -1)
            ws = (bf(hg_w_q[j]), bf(hg_w_f[j]), bf(hg_w_i[j]), bf(hg_w_g[j]))
            g_n = hg_g_norm[j].reshape(1, HG_DV)
            w_o = bf(hg_w_o[j])
            q, k, v, lf, gate = hg_proj(sp, li, *ws, lb, npre)
            s0 = jnp.zeros((B, HG_HEADS, HG_DK, HG_DV), F32)
            o, s_new = hg_scan(q, k, v, lf, gate, s0, g_n, chained=True)
            sp = sp.with_x(mm_post(sp, li, o, w_o, npost))
            hg_p.append(s_new)
            n_seq = HG_BLOCK // HG_CHUNK
            def to_chunks(a):
                a = jnp.pad(a.reshape(Bd, T, -1), ((0, 0), (HG_CHUNK - T, 0), (0, 0)))
                return a.reshape(Bd // n_seq, HG_BLOCK, -1)
            qs, ks, vs, lfs, gs = [to_chunks(a) for a in hg_proj(ss, li, *ws, lb, npre)]
            s0 = state_hgrn[j].reshape(Bd // n_seq, n_seq, HG_HEADS, HG_DK, HG_DV)
            o, s_new = hg_scan(qs, ks, vs, lfs, gs, s0, g_n, chained=False)
            o = o.reshape(Bd, HG_CHUNK, -1)[:, HG_CHUNK - T:].reshape(1, Bd * T, -1)
            ss = ss.with_x(mm_post(ss, li, o, w_o, npost))
            hg_s.append(s_new.reshape(Bd, HG_HEADS, HG_DK, HG_DV))
        else:
            w_x, w_y, w_o = bf(lru_w_x[j]), bf(lru_w_y[j]), bf(lru_w_o[j])
            W = w_x.shape[1]
            prm = (lru_conv_w[j], lru_conv_b[j].reshape(1, W), bf(lru_w_ra[j]), lru_b_ra[j].reshape(1, W),
                   bf(lru_w_ix[j]), lru_b_ix[j].reshape(1, W), lru_lam[j].reshape(1, W))
            u, gate = lru_proj(sp, li, w_x, w_y, npre)
            yg, h_last, cb = lru_scan(u, gate, jnp.zeros((B, CONV_W - 1, W), F32), jnp.zeros((B, W), F32), *prm)
            sp = sp.with_x(mm_post(sp, li, yg, w_o, npost))
            lh_p.append(h_last.reshape(B, W))
            lc_p.append(cb)
            u, gate = lru_proj(ss, li, w_x, w_y, npre)
            tm = lambda a: a.reshape(Bd, T, W).transpose(1, 0, 2)
            yg, h_last, cb = lru_step(tm(u), tm(gate), state_lru_conv[j].transpose(1, 0, 2), state_lru_h[j], *prm)
            ss = ss.with_x(mm_post(ss, li, yg.transpose(1, 0, 2).reshape(1, Bd * T, W), w_o, npost))
            lh_s.append(h_last)
            lc_s.append(cb.transpose(1, 0, 2))
        xs, w_bf = ffn(ss, li, 2, *f2, npre, npost, emit=True)
        ss = ss.with_x(xs)
        sp = sp.with_x(ffn(sp, li, 2, *w_bf, npre, npost, emit=False))

    return (sp.x, ss.x.reshape(Bd, T, D), jnp.stack(kv_p), jnp.stack(kv_s), jnp.stack(hg_p), jnp.stack(hg_s),
            jnp.stack(lh_p), jnp.stack(lh_s), jnp.stack(lc_p), jnp.stack(lc_s))
```

```python
import functools

import jax
import jax.numpy as jnp
from jax import lax
from jax.experimental import pallas as pl
from jax.experimental.pallas import tpu as pltpu

F32 = jnp.float32
BF16 = jnp.bfloat16

DEPTH = 4
N_MIXERS = 3
NORM_EPS = 1e-6
N_MOD = 9
FFN_RES = 0.5

MLA_HEADS = 16
Q_LORA = 512
KV_LORA = 512
QK_NOPE = 128
QK_ROPE = 64
V_HEAD = 128
KV_ROW = KV_LORA + QK_ROPE
KV_PAD = 640
QK_PAD = 256
MLA_SCALE = (QK_NOPE + QK_ROPE) ** -0.5
ROPE_THETA = 10000.0
PAGE_SIZE = 128
NEG_INF = -1e30

HG_HEADS = 16
HG_DK = 128
HG_DV = 128
HG_CHUNK = 16
HG_BLOCK = 128

LRU_BLOCKS = 16
LRU_BW = 128
CONV_W = 4
LRU_C = 8.0

LANE = 128
VMEM_LIMIT = 56 * 1024 * 1024


def _cparams(*sem):
    return pltpu.CompilerParams(dimension_semantics=sem, vmem_limit_bytes=VMEM_LIMIT)


def _dot(a, b):
    return jnp.dot(a, b, preferred_element_type=F32)


def _dot_nt(a, b):
    return lax.dot_general(a, b, (((1,), (1,)), ((), ())), preferred_element_type=F32)


def _rms(x, g):
    ms = jnp.mean(x * x, axis=-1, keepdims=True)
    return x * lax.rsqrt(ms + NORM_EPS) * g


def _sigmoid(x):
    return 1.0 / (1.0 + jnp.exp(-x))


def _silu(x):
    return x * _sigmoid(x)


def _softplus(x):
    return jnp.maximum(x, 0.0) + jnp.log1p(jnp.exp(-jnp.abs(x)))


def _gelu_tanh(x):
    c = 0.7978845608028654
    return 0.5 * x * (1.0 + jnp.tanh(c * (x + 0.044715 * (x * x * x))))


ROW_CHUNK = 16
ROW_UNROLL = 4


def _row_chunks(n_rows, body):
    def step(c, carry):
        body(pl.ds(pl.multiple_of(c * ROW_CHUNK, ROW_CHUNK), ROW_CHUNK))
        return carry
    lax.fori_loop(0, n_rows // ROW_CHUNK, step, 0, unroll=ROW_UNROLL)


def _mod_rows(ref, rows):
    return ref[...] if ref.shape[0] == 1 else ref[rows, :]


def _prenorm_rows(h_ref, x_ref, sh_ref, sc_ref, g_ref, rows):
    y = _rms(x_ref[rows, :], g_ref[...])
    h_ref[rows, :] = (y * (1.0 + _mod_rows(sc_ref, rows)) + _mod_rows(sh_ref, rows)).astype(BF16)


def _prenorm_into(h_ref, x_ref, sh_ref, sc_ref, g_ref):
    _row_chunks(x_ref.shape[0], functools.partial(_prenorm_rows, h_ref, x_ref, sh_ref, sc_ref, g_ref))


def _postnorm_into(o_ref, y_ref, x_ref, gt_ref, g_ref, coef):
    def body(rows):
        o_ref[rows, :] = x_ref[rows, :] + (coef * _mod_rows(gt_ref, rows)) * _rms(y_ref[rows, :], g_ref[...])
    _row_chunks(x_ref.shape[0], body)


class Stream:
    def __init__(self, x, mod, per_row):
        self.x, self.mod, self.per_row = x, mod, per_row
        self.G, self.R, self.D = x.shape
        self.bm = min(512, self.R)

    def with_x(self, x):
        return Stream(x, self.mod, self.per_row)

    def rows(self, width):
        return pl.BlockSpec((None, self.bm, width), lambda g, i, *_: (g, i, 0))

    def mod_spec(self, li, k):
        G, per_row = self.G, self.per_row
        rm = self.bm if per_row else 1
        return pl.BlockSpec((None, rm, self.D), lambda g, i, *_: (li * G + g, i if per_row else 0, k))


def _gain_spec(idx, width):
    return pl.BlockSpec((None, 1, width), lambda *_: (idx, 0, 0))


def _ada_kernel(c_ref, w_ref, b_ref, o_ref):
    cs = _silu(c_ref[...]).astype(BF16)
    o_ref[...] = _dot(cs, w_ref[...].astype(BF16)) + b_ref[...]


def ada_mod(c_all, ada_w, ada_b):
    Bc, D = c_all.shape
    N = ada_w.shape[-1]
    bn = 1024
    return pl.pallas_call(
        _ada_kernel,
        grid=(DEPTH, N // bn),
        in_specs=[pl.BlockSpec((Bc, D), lambda l, j: (0, 0)),
                  pl.BlockSpec((None, D, bn), lambda l, j: (l, 0, j)),
                  pl.BlockSpec((None, 1, bn), lambda l, j: (l, 0, j))],
        out_specs=pl.BlockSpec((None, Bc, bn), lambda l, j: (l, 0, j)),
        out_shape=jax.ShapeDtypeStruct((DEPTH, Bc, N), F32),
        compiler_params=_cparams("parallel", "parallel"),
        name="ada_mod",
    )(c_all, ada_w, ada_b.reshape(DEPTH, 1, N))


def _ffn_kernel(*refs, emit, nj):
    if emit:
        x_ref, sh_ref, sc_ref, gt_ref, gpre_ref, gpost_ref, wg_ref, wu_ref, wd_ref, o_ref = refs[:10]
    else:
        (x_ref, sh_ref, sc_ref, gt_ref, gpre_ref, gpost_ref, xn_ref, shn_ref, scn_ref,
         wg_ref, wu_ref, wd_ref, o_ref) = refs[:13]
    h_ref, acc_ref = refs[-2:]
    bm = x_ref.shape[0]
    j = pl.program_id(2)
    tile = pl.program_id(0) * pl.num_programs(1) + pl.program_id(1)
    cur = 0 if emit else tile % 2

    @pl.when((j == 0) if emit else ((j == 0) & (tile == 0)))
    def _():
        _prenorm_into(h_ref.at[cur], x_ref, sh_ref, sc_ref, gpre_ref)
        acc_ref[...] = jnp.zeros_like(acc_ref)

    if not emit:
        per_step = -(-bm // (nj * ROW_CHUNK)) * ROW_CHUNK
        start = jnp.minimum(j * per_step, bm - per_step)
        for c in range(per_step // ROW_CHUNK):
            rows = pl.ds(pl.multiple_of(start + c * ROW_CHUNK, ROW_CHUNK), ROW_CHUNK)
            _prenorm_rows(h_ref.at[1 - cur], xn_ref, shn_ref, scn_ref, gpre_ref, rows)

    wg, wu, wd = wg_ref[...].astype(BF16), wu_ref[...].astype(BF16), wd_ref[...].astype(BF16)
    if emit:
        for out_ref, w in zip(refs[10:13], (wg, wu, wd)):
            out_ref[...] = w
    h = h_ref[cur]
    g = _dot(h, wg)
    u = _dot(h, wu)
    d = _dot((_silu(g) * u).astype(BF16), wd)
    acc_ref[...] = jnp.where(j == 0, d, acc_ref[...] + d)

    @pl.when(j == nj - 1)
    def _():
        _postnorm_into(o_ref, acc_ref, x_ref, gt_ref, gpost_ref, FFN_RES)


def ffn(s, li, sub, wg, wu, wd, norm_pre, norm_post, emit):
    D, DF = wg.shape[-2:]
    k0 = 3 * sub
    G, ni, bm, per_row = s.G, s.R // s.bm, s.bm, s.per_row
    w_2d = lambda bf: [pl.BlockSpec((D, bf), lambda g, i, j: (0, j)), pl.BlockSpec((D, bf), lambda g, i, j: (0, j)),
                       pl.BlockSpec((bf, D), lambda g, i, j: (j, 0))]
    x_shape = jax.ShapeDtypeStruct(s.x.shape, F32)
    common = [s.rows(D), s.mod_spec(li, k0), s.mod_spec(li, k0 + 1), s.mod_spec(li, k0 + 2),
              _gain_spec(li * 3 + sub, D), _gain_spec(li * 3 + sub, D)]
    if emit:
        assert G * ni == 1
        bf = 256
        w_in = [pl.BlockSpec((None, D, bf), lambda g, i, j: (li, 0, j)),
                pl.BlockSpec((None, D, bf), lambda g, i, j: (li, 0, j)),
                pl.BlockSpec((None, bf, D), lambda g, i, j: (li, j, 0))]
        w_shapes = [jax.ShapeDtypeStruct((D, DF), BF16), jax.ShapeDtypeStruct((D, DF), BF16),
                    jax.ShapeDtypeStruct((DF, D), BF16)]
        out = pl.pallas_call(
            functools.partial(_ffn_kernel, emit=True, nj=DF // bf),
            grid=(G, ni, DF // bf),
            in_specs=common + w_in,
            out_specs=[s.rows(D)] + w_2d(bf),
            out_shape=[x_shape] + w_shapes,
            scratch_shapes=[pltpu.VMEM((1, bm, D), BF16), pltpu.VMEM((bm, D), F32)],
            compiler_params=_cparams("arbitrary", "arbitrary", "arbitrary"),
            name="ffn_emit",
        )(s.x, s.mod, s.mod, s.mod, norm_pre, norm_post, wg, wu, wd)
        return out[0], out[1:]

    bf = 512

    def nxt(g, i):
        t = jnp.minimum(g * ni + i + 1, G * ni - 1)
        return t // ni, t % ni

    def x_next(g, i, j):
        gn, i_n = nxt(g, i)
        return gn, i_n, 0

    def mod_next(k):
        def index(g, i, j):
            gn, i_n = nxt(g, i)
            return li * G + gn, i_n if per_row else 0, k
        return pl.BlockSpec((None, bm if per_row else 1, D), index)

    return pl.pallas_call(
        functools.partial(_ffn_kernel, emit=False, nj=DF // bf),
        grid=(G, ni, DF // bf),
        in_specs=common + [pl.BlockSpec((None, bm, D), x_next), mod_next(k0), mod_next(k0 + 1)] + w_2d(bf),
        out_specs=s.rows(D),
        out_shape=x_shape,
        scratch_shapes=[pltpu.VMEM((2, bm, D), BF16), pltpu.VMEM((bm, D), F32)],
        compiler_params=_cparams("arbitrary", "arbitrary", "arbitrary"),
        name="ffn",
    )(s.x, s.mod, s.mod, s.mod, norm_pre, norm_post, s.x, s.mod, s.mod, wg, wu, wd)


def _mm_post_kernel(a_ref, w_ref, x_ref, gt_ref, gpost_ref, o_ref):
    y = _dot(a_ref[...], w_ref[...])
    o_ref[...] = x_ref[...] + gt_ref[...] * _rms(y, gpost_ref[...])


def mm_post(s, li, a, w, norm_post):
    K, D = w.shape
    bm = min(256, s.R)
    rows = lambda width: pl.BlockSpec((None, bm, width), lambda g, i: (g, i, 0))
    G, per_row = s.G, s.per_row
    gate = pl.BlockSpec((None, bm if per_row else 1, D), lambda g, i: (li * G + g, i if per_row else 0, 5))
    return pl.pallas_call(
        _mm_post_kernel,
        grid=(s.G, s.R // bm),
        in_specs=[rows(K), pl.BlockSpec((K, D), lambda g, i: (0, 0)), rows(D), gate, _gain_spec(li * 3 + 1, D)],
        out_specs=rows(D),
        out_shape=jax.ShapeDtypeStruct(s.x.shape, F32),
        compiler_params=_cparams("parallel", "parallel"),
        name="mm_post",
    )(a, w, s.x, s.mod, norm_post)


def _mla_proj_kernel(x_ref, sh_ref, sc_ref, gpre_ref, w_ref, gq_ref, gkv_ref, cs_ref, ql_ref, kv_ref, kvb_ref):
    h = (_rms(x_ref[...], gpre_ref[...]) * (1.0 + sc_ref[...]) + sh_ref[...]).astype(BF16)
    r = _dot(h, w_ref[...])
    ql_ref[...] = _rms(r[:, :Q_LORA], gq_ref[...]).astype(BF16)
    ckv = _rms(r[:, Q_LORA:Q_LORA + KV_LORA], gkv_ref[...])
    t = r[:, Q_LORA + KV_LORA:] * cs_ref[...]
    kr = t[:, :QK_ROPE] + t[:, QK_ROPE:]
    kv_ref[:, :KV_LORA] = ckv
    kv_ref[:, KV_LORA:] = kr
    kvb_ref[:, :KV_LORA] = ckv.astype(BF16)
    lane = lax.broadcasted_iota(jnp.int32, (kr.shape[0], KV_PAD - KV_LORA), 1)
    tail = jnp.where(lane == QK_ROPE, 1.0, jnp.concatenate([kr, jnp.zeros_like(kr)], axis=-1))
    kvb_ref[:, KV_LORA:] = tail.astype(BF16)


def mla_proj(s, li, w_cat, g_q, g_kv, cs_tab, norm_pre):
    D, N = w_cat.shape
    out = (jax.ShapeDtypeStruct((s.G, s.R, Q_LORA), BF16),
           jax.ShapeDtypeStruct((s.G, s.R, KV_ROW), F32),
           jax.ShapeDtypeStruct((s.G, s.R, KV_PAD), BF16))
    return pl.pallas_call(
        _mla_proj_kernel,
        grid=(s.G, s.R // s.bm),
        in_specs=[s.rows(D), s.mod_spec(li, 3), s.mod_spec(li, 4), _gain_spec(li * 3 + 1, D),
                  pl.BlockSpec((D, N), lambda g, i: (0, 0)),
                  pl.BlockSpec((1, Q_LORA), lambda g, i: (0, 0)),
                  pl.BlockSpec((1, KV_LORA), lambda g, i: (0, 0)),
                  pl.BlockSpec((s.bm, LANE), lambda g, i: (i, 0))],
        out_specs=(s.rows(Q_LORA), s.rows(KV_ROW), s.rows(KV_PAD)),
        out_shape=out,
        compiler_params=_cparams("parallel", "parallel"),
        name="mla_proj",
    )(s.x, s.mod, s.mod, norm_pre, w_cat, g_q, g_kv, cs_tab)


def _q_expand_kernel(ql_ref, w_ref, wr_ref, cm_ref, sm_ref, q_ref, *, hb):
    ql = ql_ref[...]
    a = _dot(ql, w_ref[...])
    b = _dot(ql, wr_ref[...])
    cm, sm = cm_ref[...], sm_ref[...]
    for h in range(hb):
        sl = slice(h * QK_PAD, (h + 1) * QK_PAD)
        q_ref[:, sl] = (a[:, sl] * cm + b[:, sl] * sm).astype(BF16)


def q_expand(s, ql, w_q, w_qrot, cmap, smap):
    hb = 4
    nw = hb * QK_PAD
    wspec = pl.BlockSpec((Q_LORA, nw), lambda g, i, j: (0, j))
    tspec = pl.BlockSpec((s.bm, QK_PAD), lambda g, i, j: (i, 0))
    return pl.pallas_call(
        functools.partial(_q_expand_kernel, hb=hb),
        grid=(s.G, s.R // s.bm, MLA_HEADS // hb),
        in_specs=[s.rows(Q_LORA), wspec, wspec, tspec, tspec],
        out_specs=pl.BlockSpec((None, s.bm, nw), lambda g, i, j: (g, i, j)),
        out_shape=jax.ShapeDtypeStruct((s.G, s.R, MLA_HEADS * QK_PAD), BF16),
        compiler_params=_cparams("parallel", "parallel", "parallel"),
        name="q_expand",
    )(ql, w_q, w_qrot, cmap, smap)


def _kv_expand_kernel(kvb_ref, wk_ref, wv_ref, k_ref, v_ref):
    kvb = kvb_ref[...]
    k_ref[...] = _dot(kvb, wk_ref[...]).astype(BF16)
    v_ref[...] = _dot(kvb, wv_ref[...]).astype(BF16)


def kv_expand(s, kvb, w_k, w_v):
    hb = 4
    return pl.pallas_call(
        _kv_expand_kernel,
        grid=(s.G, s.R // s.bm, MLA_HEADS // hb),
        in_specs=[s.rows(KV_PAD),
                  pl.BlockSpec((KV_PAD, hb * QK_PAD), lambda g, i, j: (0, j)),
                  pl.BlockSpec((KV_PAD, hb * QK_PAD), lambda g, i, j: (0, j))],
        out_specs=(pl.BlockSpec((None, s.bm, hb * QK_PAD), lambda g, i, j: (g, i, j)),
                   pl.BlockSpec((None, s.bm, hb * QK_PAD), lambda g, i, j: (g, i, j))),
        out_shape=(jax.ShapeDtypeStruct((s.G, s.R, MLA_HEADS * QK_PAD), BF16),
                   jax.ShapeDtypeStruct((s.G, s.R, MLA_HEADS * QK_PAD), BF16)),
        compiler_params=_cparams("parallel", "parallel", "parallel"),
        name="kv_expand",
    )(kvb, w_k, w_v)


def _flash_kernel(q_ref, k_ref, v_ref, o_ref, m_ref, acc_ref, *, blk, hb):
    i = pl.program_id(2)
    m_ref[...] = jnp.full_like(m_ref, NEG_INF)
    acc_ref[...] = jnp.zeros_like(acc_ref)

    def block(j, diagonal):
        r0 = pl.multiple_of(j * blk, blk)
        for h in range(hb):
            hs = slice(h * QK_PAD, (h + 1) * QK_PAD)
            s = _dot_nt(q_ref[:, hs], k_ref[pl.ds(r0, blk), hs])
            if diagonal:
                row = lax.broadcasted_iota(jnp.int32, (blk, blk), 0)
                col = lax.broadcasted_iota(jnp.int32, (blk, blk), 1)
                s = jnp.where(col <= row, s, NEG_INF)
            m_prev = m_ref[h]
            m_new = jnp.maximum(m_prev, jnp.max(s, axis=-1, keepdims=True))
            alpha = jnp.exp(m_prev - m_new)
            p = jnp.exp(s - jnp.concatenate([m_new] * (blk // LANE), axis=1))
            pv = _dot(p.astype(BF16), v_ref[pl.ds(r0, blk), hs])
            acc_ref[h] = jnp.concatenate([alpha] * (QK_PAD // LANE), axis=1) * acc_ref[h] + pv
            m_ref[h] = m_new

    def below_diagonal(j, carry):
        block(j, False)
        return carry

    lax.fori_loop(0, i, below_diagonal, 0)
    block(i, True)
    for h in range(hb):
        acc = acc_ref[h]
        o_ref[:, h * V_HEAD:(h + 1) * V_HEAD] = (acc[:, :V_HEAD] / acc[:, V_HEAD:V_HEAD + 1]).astype(o_ref.dtype)


def flash_prompt(q, k, v):
    B, L, _ = q.shape
    blk = min(512, L)
    hb = 4
    seq = pl.BlockSpec((None, L, hb * QK_PAD), lambda b, h, i: (b, 0, h))
    return pl.pallas_call(
        functools.partial(_flash_kernel, blk=blk, hb=hb),
        grid=(B, MLA_HEADS // hb, L // blk),
        in_specs=[pl.BlockSpec((None, blk, hb * QK_PAD), lambda b, h, i: (b, i, h)), seq, seq],
        out_specs=pl.BlockSpec((None, blk, hb * V_HEAD), lambda b, h, i: (b, i, h)),
        out_shape=jax.ShapeDtypeStruct((B, L, MLA_HEADS * V_HEAD), BF16),
        scratch_shapes=[pltpu.VMEM((hb, blk, LANE), F32), pltpu.VMEM((hb, blk, QK_PAD), F32)],
        compiler_params=_cparams("parallel", "parallel", "parallel"),
        name="flash_prompt",
    )(q, k, v)


def _q_absorb_kernel(q_ref, w_ref, qc_ref):
    q = q_ref[...]
    qc_ref[:, :KV_LORA] = _dot(q[:, :QK_NOPE], w_ref[...]).astype(BF16)
    qc_ref[:, KV_LORA:] = q[:, QK_NOPE:]


def q_absorb(q, w_ukT):
    R = q.shape[0]
    return pl.pallas_call(
        _q_absorb_kernel,
        grid=(MLA_HEADS,),
        in_specs=[pl.BlockSpec((R, QK_PAD), lambda h: (0, h)),
                  pl.BlockSpec((None, QK_NOPE, KV_LORA), lambda h: (h, 0, 0))],
        out_specs=pl.BlockSpec((None, R, KV_PAD), lambda h: (h, 0, 0)),
        out_shape=jax.ShapeDtypeStruct((MLA_HEADS, R, KV_PAD), BF16),
        compiler_params=_cparams("parallel"),
        name="q_absorb",
    )(q, w_ukT)


PAGE_SLOTS = 3


def _paged_kernel(pt_ref, qc_ref, newt_ref, cache_ref, o_ref, buf_ref, sem_ref, kb_ref, kn_ref, m_ref, l_ref, acc_ref,
                  *, layer, pp, group, steps_per_seq, heads):
    t = pl.program_id(0)
    n_steps = pl.num_programs(0)
    j = t % steps_per_seq
    rows = qc_ref.shape[0]
    gw = group * PAGE_SIZE

    def page_copy(step, k):
        slot = step % PAGE_SLOTS
        page = pt_ref[step * pp + k]
        return pltpu.make_async_copy(cache_ref.at[layer, page], buf_ref.at[slot, k], sem_ref.at[slot])

    def fetch(step):
        for k in range(pp):
            page_copy(step, k).start()

    @pl.when(t == 0)
    def _():
        for ahead in range(PAGE_SLOTS - 1):
            @pl.when(ahead < n_steps)
            def _():
                fetch(ahead)

    @pl.when(t + (PAGE_SLOTS - 1) < n_steps)
    def _():
        fetch(t + (PAGE_SLOTS - 1))

    @pl.when(j == 0)
    def _():
        m_ref[...] = jnp.full_like(m_ref, NEG_INF)
        l_ref[...] = jnp.zeros_like(l_ref)
        acc_ref[...] = jnp.zeros_like(acc_ref)
        kb_ref[KV_ROW:, :] = jnp.zeros((KV_PAD - KV_ROW, kb_ref.shape[1]), BF16)

    def update(key_refs, visible):
        q = qc_ref[...]
        ss = [_dot(q, kr[...]) for kr in key_refs]
        if visible is not None:
            ss = [jnp.where(visible, s, NEG_INF) for s in ss]
        m_prev = m_ref[...]
        m_new = m_prev
        for s in ss:
            m_new = jnp.maximum(m_new, jnp.max(s, axis=-1, keepdims=True))
        alpha = jnp.exp(m_prev - m_new)
        l_new = alpha * l_ref[...]
        acc = alpha * acc_ref[...]
        for s, kr in zip(ss, key_refs):
            p = jnp.exp(s - m_new)
            l_new = l_new + jnp.sum(p, axis=-1, keepdims=True)
            acc = acc + _dot_nt(p.astype(BF16), kr[:KV_LORA, :])
        l_ref[...] = l_new
        acc_ref[...] = acc
        m_ref[...] = m_new

    slot = t % PAGE_SLOTS
    for k in range(pp):
        page_copy(t, k).wait()
    for k in range(pp):
        kb_ref[:KV_ROW, k * PAGE_SIZE:(k + 1) * PAGE_SIZE] = buf_ref[slot, k].astype(BF16)
    update([kb_ref.at[:, g * gw:(g + 1) * gw] for g in range(pp // group)], None)

    @pl.when(j == steps_per_seq - 1)
    def _():
        T = newt_ref.shape[1]
        kn_ref[...] = jnp.zeros_like(kn_ref)
        kn_ref[:KV_ROW, :T] = newt_ref[...].astype(BF16)
        row = lax.broadcasted_iota(jnp.int32, (rows, PAGE_SIZE), 0)
        col = lax.broadcasted_iota(jnp.int32, (rows, PAGE_SIZE), 1)
        update([kn_ref], col * heads <= row)
        o_ref[...] = acc_ref[...] / l_ref[...]


def paged_attention(qc, kv_new, cache_t, layer, page_table):
    Bd, rows, _ = qc.shape
    T = kv_new.shape[1]
    n_pages = page_table.shape[1]
    pp = 32 if n_pages % 32 == 0 else n_pages
    group = 8 if pp % 8 == 0 else pp
    sps = n_pages // pp
    grid_spec = pltpu.PrefetchScalarGridSpec(
        num_scalar_prefetch=1,
        grid=(Bd * sps,),
        in_specs=[pl.BlockSpec((None, rows, KV_PAD), lambda t, pt: (t // sps, 0, 0)),
                  pl.BlockSpec((None, KV_ROW, T), lambda t, pt: (t // sps, 0, 0)),
                  pl.BlockSpec(memory_space=pl.ANY)],
        out_specs=pl.BlockSpec((None, rows, KV_LORA), lambda t, pt: (t // sps, 0, 0)),
        scratch_shapes=[pltpu.VMEM((PAGE_SLOTS, pp, KV_ROW, PAGE_SIZE), F32), pltpu.SemaphoreType.DMA((PAGE_SLOTS,)),
                        pltpu.VMEM((KV_PAD, pp * PAGE_SIZE), BF16), pltpu.VMEM((KV_PAD, PAGE_SIZE), BF16),
                        pltpu.VMEM((rows, 1), F32), pltpu.VMEM((rows, 1), F32), pltpu.VMEM((rows, KV_LORA), F32)])
    return pl.pallas_call(
        functools.partial(_paged_kernel, layer=layer, pp=pp, group=group, steps_per_seq=sps, heads=rows // T),
        grid_spec=grid_spec,
        out_shape=jax.ShapeDtypeStruct((Bd, rows, KV_LORA), F32),
        compiler_params=_cparams("arbitrary"),
        name="paged_attention",
    )(page_table.reshape(-1), qc, kv_new.transpose(0, 2, 1), cache_t)


def _v_up_kernel(o_ref, w_ref, y_ref):
    y_ref[...] = _dot(o_ref[...].astype(BF16), w_ref[...]).astype(BF16)


def v_up(o_lat, w_uv):
    R = o_lat.shape[0]
    return pl.pallas_call(
        _v_up_kernel,
        grid=(MLA_HEADS,),
        in_specs=[pl.BlockSpec((R, KV_LORA), lambda h: (0, h)), pl.BlockSpec((KV_LORA, V_HEAD), lambda h: (0, h))],
        out_specs=pl.BlockSpec((R, V_HEAD), lambda h: (0, h)),
        out_shape=jax.ShapeDtypeStruct((R, MLA_HEADS * V_HEAD), BF16),
        compiler_params=_cparams("parallel"),
        name="v_up",
    )(o_lat, w_uv)


def _hg_proj_kernel(x_ref, sh_ref, sc_ref, gpre_ref, wq_ref, wf_ref, wi_ref, wg_ref, lb_ref,
                    q_ref, k_ref, v_ref, lf_ref, g_ref, h_ref):
    @pl.when(pl.program_id(2) == 0)
    def _():
        _prenorm_into(h_ref, x_ref, sh_ref, sc_ref, gpre_ref)

    h = h_ref[...]
    q_ref[...] = _silu(_dot(h, wq_ref[...]))
    fz = _dot(h, wf_ref[...])
    lb = lb_ref[...]
    a = jnp.log(lb)
    b = jnp.log1p(-lb) + (jnp.minimum(fz, 0.0) - jnp.log1p(jnp.exp(-jnp.abs(fz))))
    lf_ref[...] = jnp.maximum(a, b) + jnp.log1p(jnp.exp(-jnp.abs(a - b)))
    k_ref[...] = (1.0 - lb) * _sigmoid(-fz)
    v_ref[...] = _dot(h, wi_ref[...])
    g_ref[...] = _silu(_dot(h, wg_ref[...]))


def hg_proj(s, li, w_q, w_f, w_i, w_g, lb, norm_pre):
    D, N = w_q.shape
    bn = 512
    wspec = pl.BlockSpec((D, bn), lambda g, i, j: (0, j))
    ospec = pl.BlockSpec((None, s.bm, bn), lambda g, i, j: (g, i, j))
    oshape = jax.ShapeDtypeStruct((s.G, s.R, N), F32)
    return pl.pallas_call(
        _hg_proj_kernel,
        grid=(s.G, s.R // s.bm, N // bn),
        in_specs=[s.rows(D), s.mod_spec(li, 3), s.mod_spec(li, 4), _gain_spec(li * 3 + 1, D),
                  wspec, wspec, wspec, wspec, pl.BlockSpec((1, bn), lambda g, i, j: (0, j))],
        out_specs=(ospec,) * 5,
        out_shape=(oshape,) * 5,
        scratch_shapes=[pltpu.VMEM((s.bm, D), BF16)],
        compiler_params=_cparams("parallel", "parallel", "arbitrary"),
        name="hg_proj",
    )(s.x, s.mod, s.mod, norm_pre, w_q, w_f, w_i, w_g, lb)


def _cumsum_rows(x, span):
    row = lax.broadcasted_iota(jnp.int32, x.shape, 0)
    pos = jnp.bitwise_and(row, span - 1)
    sh = 1
    while sh < span:
        x = x + jnp.where(pos >= sh, pltpu.roll(x, sh, 0), 0.0)
        sh *= 2
    return x


def _chunk_id(r):
    return lax.shift_right_logical(r, HG_CHUNK.bit_length() - 1)


def _hg_scan_kernel(q_ref, k_ref, v_ref, lf_ref, g_ref, s0_ref, gn_ref, o_ref, so_ref, st_ref, *, hb, chained):
    n = HG_BLOCK // HG_CHUNK
    first = pl.program_id(2) == 0
    last = pl.program_id(2) == pl.num_programs(2) - 1

    if chained:
        @pl.when(first)
        def _():
            for h in range(hb):
                st_ref[h] = s0_ref[h].T

    row = lax.broadcasted_iota(jnp.int32, (HG_BLOCK, HG_BLOCK), 0)
    col = lax.broadcasted_iota(jnp.int32, (HG_BLOCK, HG_BLOCK), 1)
    causal = (_chunk_id(row) == _chunk_id(col)) & (col <= row)
    chunk_of_row = _chunk_id(lax.broadcasted_iota(jnp.int32, (HG_BLOCK, HG_DK), 0))

    for h in range(hb):
        sl = slice(h * HG_DK, (h + 1) * HG_DK)
        b = _cumsum_rows(lf_ref[:, sl], HG_CHUNK)
        b_last = jnp.concatenate(
            [jnp.broadcast_to(b[c * HG_CHUNK + HG_CHUNK - 1:(c + 1) * HG_CHUNK, :], (HG_CHUNK, HG_DK)) for c in range(n)],
            axis=0)
        q, k, v = q_ref[:, sl], k_ref[:, sl], v_ref[:, sl]
        qg = q * jnp.exp(b)
        kd = (k * jnp.exp(-b)).astype(BF16)
        kl = k * jnp.exp(b_last - b)
        qgb = qg.astype(BF16)
        vb = v.astype(BF16)
        a = jnp.where(causal, _dot_nt(qgb, kd), 0.0)
        o = _dot(a.astype(BF16), vb)
        klx = jnp.concatenate([jnp.where(chunk_of_row == c, kl, 0.0).astype(BF16) for c in range(n)], axis=1)
        ut = _dot(v.T.astype(BF16), klx)
        sts = []
        if chained:
            st = st_ref[h]
        for c in range(n):
            if not chained:
                st = s0_ref[c, h].T
            sts.append(st.astype(BF16))
            dec = jnp.exp(b_last[c * HG_CHUNK:c * HG_CHUNK + 1, :])
            st = st * dec + ut[:, c * HG_DK:(c + 1) * HG_DK]
            if not chained:
                so_ref[c, h] = st.T
        if chained:
            st_ref[h] = st

            @pl.when(last)
            def _():
                so_ref[h] = st.T
        qx = jnp.concatenate([jnp.where(chunk_of_row == c, qg, 0.0).astype(BF16) for c in range(n)], axis=1)
        o = o + _dot_nt(qx, jnp.concatenate(sts, axis=1))
        o_ref[:, sl] = (_rms(o, gn_ref[...]) * g_ref[:, sl]).astype(BF16)


def hg_scan(q, k, v, lf, gate, s0, g_norm, chained):
    G, R, N = q.shape
    hb = 4 if chained else 2
    n = HG_BLOCK // HG_CHUNK
    rows = pl.BlockSpec((None, HG_BLOCK, hb * HG_DK), lambda g, h, l: (g, l, h))
    if chained:
        sspec = pl.BlockSpec((None, hb, HG_DK, HG_DV), lambda g, h, l: (g, h, 0, 0))
    else:
        sspec = pl.BlockSpec((None, n, hb, HG_DK, HG_DV), lambda g, h, l: (g, 0, h, 0, 0))
    return pl.pallas_call(
        functools.partial(_hg_scan_kernel, hb=hb, chained=chained),
        grid=(G, HG_HEADS // hb, R // HG_BLOCK),
        in_specs=[rows, rows, rows, rows, rows, sspec, pl.BlockSpec((1, HG_DV), lambda g, h, l: (0, 0))],
        out_specs=(rows, sspec),
        out_shape=(jax.ShapeDtypeStruct((G, R, N), BF16), jax.ShapeDtypeStruct(s0.shape, F32)),
        scratch_shapes=[pltpu.VMEM((hb, HG_DV, HG_DK), F32)],
        compiler_params=_cparams("parallel", "parallel", "arbitrary"),
        name="hg_scan",
    )(q, k, v, lf, gate, s0, g_norm)


def _lru_proj_kernel(x_ref, sh_ref, sc_ref, gpre_ref, wx_ref, wy_ref, u_ref, g_ref, h_ref):
    @pl.when(pl.program_id(2) == 0)
    def _():
        _prenorm_into(h_ref, x_ref, sh_ref, sc_ref, gpre_ref)

    h = h_ref[...]
    u_ref[...] = _dot(h, wx_ref[...])
    g_ref[...] = _gelu_tanh(_dot(h, wy_ref[...]))


def lru_proj(s, li, w_x, w_y, norm_pre):
    D, N = w_x.shape
    bn = 512
    wspec = pl.BlockSpec((D, bn), lambda g, i, j: (0, j))
    ospec = pl.BlockSpec((None, s.bm, bn), lambda g, i, j: (g, i, j))
    oshape = jax.ShapeDtypeStruct((s.G, s.R, N), F32)
    return pl.pallas_call(
        _lru_proj_kernel,
        grid=(s.G, s.R // s.bm, N // bn),
        in_specs=[s.rows(D), s.mod_spec(li, 3), s.mod_spec(li, 4), _gain_spec(li * 3 + 1, D), wspec, wspec],
        out_specs=(ospec, ospec),
        out_shape=(oshape, oshape),
        scratch_shapes=[pltpu.VMEM((s.bm, D), BF16)],
        compiler_params=_cparams("parallel", "parallel", "arbitrary"),
        name="lru_proj",
    )(s.x, s.mod, s.mod, norm_pre, w_x, w_y)


def _lru_gates(xc, n, wra_ref, bra_ref, wix_ref, bix_ref, lam_ref):
    sl = slice(n * LRU_BW, (n + 1) * LRU_BW)
    xb = xc.astype(BF16)
    r = _sigmoid(_dot(xb, wra_ref[n]) + bra_ref[:, sl])
    ig = _sigmoid(_dot(xb, wix_ref[n]) + bix_ref[:, sl])
    log_a = -LRU_C * r * _softplus(-lam_ref[:, sl])
    a = jnp.exp(log_a)
    return a, jnp.sqrt(-jnp.tanh(log_a) * (a * a + 1.0)) * (ig * xc)


def _lru_scan_kernel(u_ref, gate_ref, cb_ref, h0_ref, cw_ref, cbias_ref, wra_ref, bra_ref, wix_ref, bix_ref, lam_ref,
                     yg_ref, hl_ref, cbo_ref, ext_ref, a_ref, b_ref, hc_ref, *, bl):
    l = pl.program_id(1)
    keep = CONV_W - 1
    base = 8

    @pl.when(l == 0)
    def _():
        ext_ref[base - keep:base, :] = cb_ref[...]
        hc_ref[...] = h0_ref[...]

    ext_ref[base:base + bl, :] = u_ref[...]
    for n in range(LRU_BLOCKS):
        sl = slice(n * LRU_BW, (n + 1) * LRU_BW)
        xc = cbias_ref[:, sl]
        for j in range(CONV_W):
            xc = xc + ext_ref[base - keep + j:base - keep + j + bl, sl] * cw_ref[j:j + 1, sl]
        a, b = _lru_gates(xc, n, wra_ref, bra_ref, wix_ref, bix_ref, lam_ref)
        a_ref[:, sl] = a
        b_ref[:, sl] = b

    pos = lax.broadcasted_iota(jnp.int32, (8, a_ref.shape[1]), 0)

    def tile(t, h_prev):
        r0 = pl.multiple_of(t * 8, 8)
        a_t = a_ref[pl.ds(r0, 8), :]
        b_t = b_ref[pl.ds(r0, 8), :]
        for sh in (1, 2, 4):
            m = pos >= sh
            b_t = jnp.where(m, a_t * pltpu.roll(b_t, sh, 0) + b_t, b_t)
            a_t = jnp.where(m, a_t * pltpu.roll(a_t, sh, 0), a_t)
        h_t = b_t + a_t * h_prev
        b_ref[pl.ds(r0, 8), :] = h_t
        return h_t[7:8, :]

    h_last = lax.fori_loop(0, bl // 8, tile, hc_ref[...])
    hc_ref[...] = h_last
    yg_ref[...] = (b_ref[...] * gate_ref[...]).astype(BF16)
    tail = ext_ref[base + bl - keep:base + bl, :]
    ext_ref[base - keep:base, :] = tail

    @pl.when(l == pl.num_programs(1) - 1)
    def _():
        hl_ref[...] = h_last
        cbo_ref[...] = tail


def lru_scan(u, gate, conv_buf, h0, conv_w, conv_b, w_ra, b_ra, w_ix, b_ix, lam):
    B, L, W = u.shape
    bl = min(256, L)
    rows = pl.BlockSpec((None, bl, W), lambda b, l: (b, l, 0))
    vec = pl.BlockSpec((1, W), lambda b, l: (0, 0))
    wblk = pl.BlockSpec((LRU_BLOCKS, LRU_BW, LRU_BW), lambda b, l: (0, 0, 0))
    hspec = pl.BlockSpec((None, 1, W), lambda b, l: (b, 0, 0))
    cspec = pl.BlockSpec((None, CONV_W - 1, W), lambda b, l: (b, 0, 0))
    return pl.pallas_call(
        functools.partial(_lru_scan_kernel, bl=bl),
        grid=(B, L // bl),
        in_specs=[rows, rows, cspec, hspec, pl.BlockSpec((CONV_W, W), lambda b, l: (0, 0)), vec,
                  wblk, vec, wblk, vec, vec],
        out_specs=(rows, hspec, cspec),
        out_shape=(jax.ShapeDtypeStruct((B, L, W), BF16), jax.ShapeDtypeStruct((B, 1, W), F32),
                   jax.ShapeDtypeStruct((B, CONV_W - 1, W), F32)),
        scratch_shapes=[pltpu.VMEM((bl + 8, W), F32), pltpu.VMEM((bl, W), F32), pltpu.VMEM((bl, W), F32),
                        pltpu.VMEM((1, W), F32)],
        compiler_params=_cparams("parallel", "arbitrary"),
        name="lru_scan",
    )(u, gate, conv_buf, h0.reshape(B, 1, W), conv_w, conv_b, w_ra, b_ra, w_ix, b_ix, lam)


def _lru_step_kernel(u_ref, gate_ref, cb_ref, h0_ref, cw_ref, cbias_ref, wra_ref, bra_ref, wix_ref, bix_ref, lam_ref,
                     yg_ref, hl_ref, cbo_ref):
    T = u_ref.shape[0]
    keep = CONV_W - 1
    for n in range(LRU_BLOCKS):
        sl = slice(n * LRU_BW, (n + 1) * LRU_BW)
        ext = [cb_ref[j, :, sl] for j in range(keep)] + [u_ref[t, :, sl] for t in range(T)]
        h = h0_ref[:, sl]
        for t in range(T):
            xc = cbias_ref[:, sl]
            for j in range(CONV_W):
                xc = xc + ext[t + j] * cw_ref[j:j + 1, sl]
            a, b = _lru_gates(xc, n, wra_ref, bra_ref, wix_ref, bix_ref, lam_ref)
            h = a * h + b
            yg_ref[t, :, sl] = (h * gate_ref[t, :, sl]).astype(BF16)
        hl_ref[:, sl] = h
        for j in range(keep):
            cbo_ref[j, :, sl] = ext[T + j]


def lru_step(u, gate, conv_buf, h0, conv_w, conv_b, w_ra, b_ra, w_ix, b_ix, lam):
    T, Bd, W = u.shape
    full = lambda *shape: pl.BlockSpec(shape, lambda i: (0,) * len(shape))
    return pl.pallas_call(
        _lru_step_kernel,
        grid=(1,),
        in_specs=[full(T, Bd, W), full(T, Bd, W), full(CONV_W - 1, Bd, W), full(Bd, W), full(CONV_W, W), full(1, W),
                  full(LRU_BLOCKS, LRU_BW, LRU_BW), full(1, W), full(LRU_BLOCKS, LRU_BW, LRU_BW), full(1, W), full(1, W)],
        out_specs=(full(T, Bd, W), full(Bd, W), full(CONV_W - 1, Bd, W)),
        out_shape=(jax.ShapeDtypeStruct((T, Bd, W), BF16), jax.ShapeDtypeStruct((Bd, W), F32),
                   jax.ShapeDtypeStruct((CONV_W - 1, Bd, W), F32)),
        compiler_params=_cparams("arbitrary"),
        name="lru_step",
    )(u, gate, conv_buf, h0, conv_w, conv_b, w_ra, b_ra, w_ix, b_ix, lam)


def _rope_tables(pos):
    half = QK_ROPE // 2
    inv = ROPE_THETA ** (-jnp.arange(half, dtype=F32) / half)
    ang = pos.astype(F32)[:, None] * inv[None, :]
    cos, sin = jnp.cos(ang), jnp.sin(ang)
    cos2, sin2 = jnp.concatenate([cos, cos], -1), jnp.concatenate([sin, sin], -1)
    ones, zeros = jnp.ones((pos.shape[0], QK_NOPE), F32), jnp.zeros((pos.shape[0], QK_NOPE), F32)
    pad = jnp.zeros((pos.shape[0], QK_PAD - QK_NOPE - QK_ROPE), F32)
    cs_tab = jnp.concatenate([cos2, sin2], -1)
    cmap = jnp.concatenate([ones, cos2, pad], -1) * MLA_SCALE
    smap = jnp.concatenate([zeros, sin2, pad], -1) * MLA_SCALE
    return cs_tab, cmap, smap


def _rot_cols(w):
    half = w.shape[-1] // 2
    return jnp.concatenate([-w[..., half:], w[..., :half]], -1)


def _mla_weights(w_dq, w_uq, w_dkv, w_uk, w_uv):
    kr = w_dkv[:, KV_LORA:]
    w_cat = jnp.concatenate([w_dq, w_dkv, _rot_cols(kr)], axis=1).astype(BF16)
    uq = w_uq.reshape(Q_LORA, MLA_HEADS, QK_NOPE + QK_ROPE)
    nope, rope = uq[..., :QK_NOPE], uq[..., QK_NOPE:]
    pad = jnp.zeros((Q_LORA, MLA_HEADS, QK_PAD - QK_NOPE - QK_ROPE), F32)
    w_q = jnp.concatenate([nope, rope, pad], -1).reshape(Q_LORA, MLA_HEADS * QK_PAD).astype(BF16)
    w_qrot = jnp.concatenate([jnp.zeros_like(nope), _rot_cols(rope), pad], -1)
    w_qrot = w_qrot.reshape(Q_LORA, MLA_HEADS * QK_PAD).astype(BF16)
    w_k = jnp.zeros((KV_PAD, MLA_HEADS, QK_PAD), F32)
    w_k = w_k.at[:KV_LORA, :, :QK_NOPE].set(w_uk)
    eye = jnp.broadcast_to(jnp.eye(QK_ROPE, dtype=F32)[:, None, :], (QK_ROPE, MLA_HEADS, QK_ROPE))
    w_k = w_k.at[KV_LORA:KV_ROW, :, QK_NOPE:QK_NOPE + QK_ROPE].set(eye)
    w_k = w_k.reshape(KV_PAD, MLA_HEADS * QK_PAD).astype(BF16)
    w_v = jnp.zeros((KV_PAD, MLA_HEADS, QK_PAD), F32)
    w_v = w_v.at[:KV_LORA, :, :V_HEAD].set(w_uv).at[KV_ROW, :, V_HEAD].set(1.0)
    w_v = w_v.reshape(KV_PAD, MLA_HEADS * QK_PAD).astype(BF16)
    w_ukT = w_uk.transpose(1, 2, 0).astype(BF16)
    w_uv2 = w_uv.reshape(KV_LORA, MLA_HEADS * V_HEAD).astype(BF16)
    return w_cat, w_q, w_qrot, w_k, w_v, w_ukT, w_uv2


def _hg_lower_bounds(lb_param):
    cs = jnp.cumsum(jax.nn.softmax(lb_param.astype(F32), axis=0), axis=0)
    return cs - cs[0]


def kernel(x_prompt, x_sample, c_prompt, c_sample, cache_mla_kv, page_table, state_hgrn, state_lru_h, state_lru_conv, ada_w, ada_b, norm_pre, norm_post, ffn1_wg, ffn1_wu, ffn1_wd, ffn2_wg, ffn2_wu, ffn2_wd, mla_w_dq, mla_g_q, mla_w_uq, mla_w_dkv, mla_g_kv, mla_w_uk, mla_w_uv, mla_w_o, hg_lb, hg_w_q, hg_w_f, hg_w_i, hg_w_g, hg_g_norm, hg_w_o, lru_w_x, lru_w_y, lru_conv_w, lru_conv_b, lru_w_ra, lru_b_ra, lru_w_ix, lru_b_ix, lru_lam, lru_w_o):
    B, L, D = x_prompt.shape
    Bd, T, _ = x_sample.shape
    n_pages = page_table.shape[1]
    past_len = n_pages * PAGE_SIZE

    n_s = Bd * T
    c_all = jnp.concatenate([jnp.repeat(c_sample, T, axis=0), c_prompt, jnp.zeros((-(n_s + B) % 8, D), F32)], axis=0)
    mod = ada_mod(c_all, ada_w, ada_b)
    mod_p = mod[:, n_s:n_s + B].reshape(DEPTH * B, 1, N_MOD * D)
    sp = Stream(x_prompt, mod_p, per_row=False)
    ss = Stream(x_sample.reshape(1, n_s, D), mod, per_row=True)

    npre = norm_pre.reshape(DEPTH * 3, 1, D)
    npost = norm_post.reshape(DEPTH * 3, 1, D)
    bf = lambda w: w.astype(BF16)
    f1 = (ffn1_wg, ffn1_wu, ffn1_wd)
    f2 = (ffn2_wg, ffn2_wu, ffn2_wd)
    cache_t = jnp.swapaxes(cache_mla_kv, 2, 3)

    tab_p = _rope_tables(jnp.arange(L))
    tab_s = _rope_tables(jnp.tile(past_len + jnp.arange(T), Bd))
    lbs = _hg_lower_bounds(hg_lb)

    kv_p, kv_s, hg_p, hg_s, lh_p, lh_s, lc_p, lc_s = [], [], [], [], [], [], [], []
    for li in range(DEPTH):
        xs, w_bf = ffn(ss, li, 0, *f1, npre, npost, emit=True)
        ss = ss.with_x(xs)
        sp = sp.with_x(ffn(sp, li, 0, *w_bf, npre, npost, emit=False))
        kind, j = li % N_MIXERS, li // N_MIXERS
        if kind == 0:
            w_cat, w_q, w_qrot, w_k, w_v, w_ukT, w_uv2 = _mla_weights(
                mla_w_dq[j], mla_w_uq[j], mla_w_dkv[j], mla_w_uk[j], mla_w_uv[j])
            g_q, g_kv = mla_g_q[j].reshape(1, Q_LORA), mla_g_kv[j].reshape(1, KV_LORA)
            w_o = bf(mla_w_o[j])
            ql, kvr, kvb = mla_proj(sp, li, w_cat, g_q, g_kv, tab_p[0], npre)
            q = q_expand(sp, ql, w_q, w_qrot, tab_p[1], tab_p[2])
            kk, vv = kv_expand(sp, kvb, w_k, w_v)
            o = flash_prompt(q, kk, vv)
            sp = sp.with_x(mm_post(sp, li, o, w_o, npost))
            kv_p.append(kvr)
            ql, kvr, _ = mla_proj(ss, li, w_cat, g_q, g_kv, tab_s[0], npre)
            q = q_expand(ss, ql, w_q, w_qrot, tab_s[1], tab_s[2])
            qc = q_absorb(q[0], w_ukT)
            qc = qc.reshape(MLA_HEADS, Bd, T, KV_PAD).transpose(1, 2, 0, 3).reshape(Bd, T * MLA_HEADS, KV_PAD)
            kv_new = kvr.reshape(Bd, T, KV_ROW)
            o_lat = paged_attention(qc, kv_new, cache_t, j, page_table)
            o = v_up(o_lat.reshape(Bd * T, MLA_HEADS * KV_LORA), w_uv2)
            ss = ss.with_x(mm_post(ss, li, o.reshape(1, Bd * T, -1), w_o, npost))
            kv_s.append(kv_new)
        elif kind == 1:
            lb = lbs[li].reshape(1, -1)
            ws = (bf(hg_w_q[j]), bf(hg_w_f[j]), bf(hg_w_i[j]), bf(hg_w_g[j]))
            g_n = hg_g_norm[j].reshape(1, HG_DV)
            w_o = bf(hg_w_o[j])
            q, k, v, lf, gate = hg_proj(sp, li, *ws, lb, npre)
            s0 = jnp.zeros((B, HG_HEADS, HG_DK, HG_DV), F32)
            o, s_new = hg_scan(q, k, v, lf, gate, s0, g_n, chained=True)
            sp = sp.with_x(mm_post(sp, li, o, w_o, npost))
            hg_p.append(s_new)
            n_seq = HG_BLOCK // HG_CHUNK
            def to_chunks(a):
                a = jnp.pad(a.reshape(Bd, T, -1), ((0, 0), (HG_CHUNK - T, 0), (0, 0)))
                return a.reshape(Bd // n_seq, HG_BLOCK, -1)
            qs, ks, vs, lfs, gs = [to_chunks(a) for a in hg_proj(ss, li, *ws, lb, npre)]
            s0 = state_hgrn[j].reshape(Bd // n_seq, n_seq, HG_HEADS, HG_DK, HG_DV)
            o, s_new = hg_scan(qs, ks, vs, lfs, gs, s0, g_n, chained=False)
            o = o.reshape(Bd, HG_CHUNK, -1)[:, HG_CHUNK - T:].reshape(1, Bd * T, -1)
            ss = ss.with_x(mm_post(ss, li, o, w_o, npost))
            hg_s.append(s_new.reshape(Bd, HG_HEADS, HG_DK, HG_DV))
        else:
            w_x, w_y, w_o = bf(lru_w_x[j]), bf(lru_w_y[j]), bf(lru_w_o[j])
            W = w_x.shape[1]
            prm = (lru_conv_w[j], lru_conv_b[j].reshape(1, W), bf(lru_w_ra[j]), lru_b_ra[j].reshape(1, W),
                   bf(lru_w_ix[j]), lru_b_ix[j].reshape(1, W), lru_lam[j].reshape(1, W))
            u, gate = lru_proj(sp, li, w_x, w_y, npre)
            yg, h_last, cb = lru_scan(u, gate, jnp.zeros((B, CONV_W - 1, W), F32), jnp.zeros((B, W), F32), *prm)
            sp = sp.with_x(mm_post(sp, li, yg, w_o, npost))
            lh_p.append(h_last.reshape(B, W))
            lc_p.append(cb)
            u, gate = lru_proj(ss, li, w_x, w_y, npre)
            tm = lambda a: a.reshape(Bd, T, W).transpose(1, 0, 2)
            yg, h_last, cb = lru_step(tm(u), tm(gate), state_lru_conv[j].transpose(1, 0, 2), state_lru_h[j], *prm)
            ss = ss.with_x(mm_post(ss, li, yg.transpose(1, 0, 2).reshape(1, Bd * T, W), w_o, npost))
            lh_s.append(h_last)
            lc_s.append(cb.transpose(1, 0, 2))
        xs, w_bf = ffn(ss, li, 2, *f2, npre, npost, emit=True)
        ss = ss.with_x(xs)
        sp = sp.with_x(ffn(sp, li, 2, *w_bf, npre, npost, emit=False))

    return (sp.x, ss.x.reshape(Bd, T, D), jnp.stack(kv_p), jnp.stack(kv_s), jnp.stack(hg_p), jnp.stack(hg_s),
            jnp.stack(lh_p), jnp.stack(lh_s), jnp.stack(lc_p), jnp.stack(lc_s))
```

```python
import functools

import jax
import jax.numpy as jnp
from jax import lax
from jax.experimental import pallas as pl
from jax.experimental.pallas import tpu as pltpu

F32 = jnp.float32
BF16 = jnp.bfloat16

DEPTH = 4
N_MIXERS = 3
NORM_EPS = 1e-6
N_MOD = 9
FFN_RES = 0.5

MLA_HEADS = 16
Q_LORA = 512
KV_LORA = 512
QK_NOPE = 128
QK_ROPE = 64
V_HEAD = 128
KV_ROW = KV_LORA + QK_ROPE
KV_PAD = 640
QK_PAD = 256
MLA_SCALE = (QK_NOPE + QK_ROPE) ** -0.5
ROPE_THETA = 10000.0
PAGE_SIZE = 128
NEG_INF = -1e30

HG_HEADS = 16
HG_DK = 128
HG_DV = 128
HG_CHUNK = 16
HG_BLOCK = 128

LRU_BLOCKS = 16
LRU_BW = 128
CONV_W = 4
LRU_C = 8.0

LANE = 128
VMEM_LIMIT = 56 * 1024 * 1024


def _cparams(*sem):
    return pltpu.CompilerParams(dimension_semantics=sem, vmem_limit_bytes=VMEM_LIMIT)


def _dot(a, b):
    return jnp.dot(a, b, preferred_element_type=F32)


def _dot_nt(a, b):
    return lax.dot_general(a, b, (((1,), (1,)), ((), ())), preferred_element_type=F32)


def _rms(x, g):
    ms = jnp.mean(x * x, axis=-1, keepdims=True)
    return x * lax.rsqrt(ms + NORM_EPS) * g


def _sigmoid(x):
    return 1.0 / (1.0 + jnp.exp(-x))


def _silu(x):
    return x * _sigmoid(x)


def _softplus(x):
    return jnp.maximum(x, 0.0) + jnp.log1p(jnp.exp(-jnp.abs(x)))


def _gelu_tanh(x):
    c = 0.7978845608028654
    return 0.5 * x * (1.0 + jnp.tanh(c * (x + 0.044715 * (x * x * x))))


ROW_CHUNK = 16
ROW_UNROLL = 4


def _row_chunks(n_rows, body):
    def step(c, carry):
        body(pl.ds(pl.multiple_of(c * ROW_CHUNK, ROW_CHUNK), ROW_CHUNK))
        return carry
    lax.fori_loop(0, n_rows // ROW_CHUNK, step, 0, unroll=ROW_UNROLL)


def _mod_rows(ref, rows):
    return ref[...] if ref.shape[0] == 1 else ref[rows, :]


def _prenorm_rows(h_ref, x_ref, sh_ref, sc_ref, g_ref, rows):
    y = _rms(x_ref[rows, :], g_ref[...])
    h_ref[rows, :] = (y * (1.0 + _mod_rows(sc_ref, rows)) + _mod_rows(sh_ref, rows)).astype(BF16)


def _prenorm_into(h_ref, x_ref, sh_ref, sc_ref, g_ref):
    _row_chunks(x_ref.shape[0], functools.partial(_prenorm_rows, h_ref, x_ref, sh_ref, sc_ref, g_ref))


def _postnorm_into(o_ref, y_ref, x_ref, gt_ref, g_ref, coef):
    def body(rows):
        o_ref[rows, :] = x_ref[rows, :] + (coef * _mod_rows(gt_ref, rows)) * _rms(y_ref[rows, :], g_ref[...])
    _row_chunks(x_ref.shape[0], body)


class Stream:
    def __init__(self, x, mod, per_row, bm=None):
        self.x, self.mod, self.per_row = x, mod, per_row
        self.G, self.R, self.D = x.shape
        self.bm = min(512, self.R) if bm is None else bm

    def with_x(self, x):
        return Stream(x, self.mod, self.per_row)

    def rows(self, width):
        return pl.BlockSpec((None, self.bm, width), lambda g, i, *_: (g, i, 0))

    def mod_spec(self, li, k):
        G, per_row = self.G, self.per_row
        rm = self.bm if per_row else 1
        return pl.BlockSpec((None, rm, self.D), lambda g, i, *_: (li * G + g, i if per_row else 0, k))


def _gain_spec(idx, width):
    return pl.BlockSpec((None, 1, width), lambda *_: (idx, 0, 0))


def _ada_kernel(c_ref, w_ref, b_ref, o_ref):
    cs = _silu(c_ref[...]).astype(BF16)
    o_ref[...] = _dot(cs, w_ref[...].astype(BF16)) + b_ref[...]


def ada_mod(c_all, ada_w, ada_b):
    Bc, D = c_all.shape
    N = ada_w.shape[-1]
    bn = 1024
    return pl.pallas_call(
        _ada_kernel,
        grid=(DEPTH, N // bn),
        in_specs=[pl.BlockSpec((Bc, D), lambda l, j: (0, 0)),
                  pl.BlockSpec((None, D, bn), lambda l, j: (l, 0, j)),
                  pl.BlockSpec((None, 1, bn), lambda l, j: (l, 0, j))],
        out_specs=pl.BlockSpec((None, Bc, bn), lambda l, j: (l, 0, j)),
        out_shape=jax.ShapeDtypeStruct((DEPTH, Bc, N), F32),
        compiler_params=_cparams("parallel", "parallel"),
        name="ada_mod",
    )(c_all, ada_w, ada_b.reshape(DEPTH, 1, N))


FFN_BF = 256
FFN_BM = 1024


def _ffn_kernel(x_ref, sh_ref, sc_ref, gt_ref, gpre_ref, gpost_ref, wg_ref, wu_ref, wd_ref, o_ref, *rest, emit):
    h_ref, acc_ref = rest[-2:]
    j = pl.program_id(2)

    @pl.when(j == 0)
    def _():
        _prenorm_into(h_ref, x_ref, sh_ref, sc_ref, gpre_ref)
        acc_ref[...] = jnp.zeros_like(acc_ref)

    wg, wu, wd = wg_ref[...].astype(BF16), wu_ref[...].astype(BF16), wd_ref[...].astype(BF16)
    if emit:
        for out_ref, w in zip(rest[:3], (wg, wu, wd)):
            out_ref[...] = w
    h = h_ref[...]
    g = _dot(h, wg)
    u = _dot(h, wu)
    acc_ref[...] += _dot((_silu(g) * u).astype(BF16), wd)

    @pl.when(j == pl.num_programs(2) - 1)
    def _():
        _postnorm_into(o_ref, acc_ref, x_ref, gt_ref, gpost_ref, FFN_RES)


def ffn(s, li, sub, wg, wu, wd, norm_pre, norm_post, emit):
    D = s.D
    DF = wd.shape[-2]
    k0 = 3 * sub
    bf = FFN_BF
    nj = DF // bf
    if not emit:
        s = Stream(s.x, s.mod, s.per_row, bm=min(FFN_BM, s.R))
    blocked = [pl.BlockSpec((None, D, bf), lambda g, i, j: (j, 0, 0)), pl.BlockSpec((None, D, bf), lambda g, i, j: (j, 0, 0)),
               pl.BlockSpec((bf, D), lambda g, i, j: (j, 0))]
    x_shape = jax.ShapeDtypeStruct(s.x.shape, F32)
    common = [s.rows(D), s.mod_spec(li, k0), s.mod_spec(li, k0 + 1), s.mod_spec(li, k0 + 2),
              _gain_spec(li * 3 + sub, D), _gain_spec(li * 3 + sub, D)]
    if emit:
        assert s.G * (s.R // s.bm) == 1
        w_in = [pl.BlockSpec((None, D, bf), lambda g, i, j: (li, 0, j)),
                pl.BlockSpec((None, D, bf), lambda g, i, j: (li, 0, j)),
                pl.BlockSpec((None, bf, D), lambda g, i, j: (li, j, 0))]
        w_shapes = [jax.ShapeDtypeStruct((nj, D, bf), BF16), jax.ShapeDtypeStruct((nj, D, bf), BF16),
                    jax.ShapeDtypeStruct((DF, D), BF16)]
    out = pl.pallas_call(
        functools.partial(_ffn_kernel, emit=emit),
        grid=(s.G, s.R // s.bm, nj),
        in_specs=common + (w_in if emit else blocked),
        out_specs=[s.rows(D)] + blocked if emit else s.rows(D),
        out_shape=[x_shape] + w_shapes if emit else x_shape,
        scratch_shapes=[pltpu.VMEM((s.bm, D), BF16), pltpu.VMEM((s.bm, D), F32)],
        compiler_params=_cparams("parallel", "parallel", "arbitrary"),
        name="ffn_emit" if emit else "ffn",
    )(s.x, s.mod, s.mod, s.mod, norm_pre, norm_post, wg, wu, wd)
    return (out[0], out[1:]) if emit else out


def _mm_post_kernel(a_ref, w_ref, x_ref, gt_ref, gpost_ref, o_ref):
    y = _dot(a_ref[...], w_ref[...])
    o_ref[...] = x_ref[...] + gt_ref[...] * _rms(y, gpost_ref[...])


def mm_post(s, li, a, w, norm_post):
    K, D = w.shape
    bm = min(256, s.R)
    rows = lambda width: pl.BlockSpec((None, bm, width), lambda g, i: (g, i, 0))
    G, per_row = s.G, s.per_row
    gate = pl.BlockSpec((None, bm if per_row else 1, D), lambda g, i: (li * G + g, i if per_row else 0, 5))
    return pl.pallas_call(
        _mm_post_kernel,
        grid=(s.G, s.R // bm),
        in_specs=[rows(K), pl.BlockSpec((K, D), lambda g, i: (0, 0)), rows(D), gate, _gain_spec(li * 3 + 1, D)],
        out_specs=rows(D),
        out_shape=jax.ShapeDtypeStruct(s.x.shape, F32),
        compiler_params=_cparams("parallel", "parallel"),
        name="mm_post",
    )(a, w, s.x, s.mod, norm_post)


def _mla_proj_kernel(x_ref, sh_ref, sc_ref, gpre_ref, w_ref, gq_ref, gkv_ref, cs_ref, ql_ref, kv_ref, kvb_ref):
    h = (_rms(x_ref[...], gpre_ref[...]) * (1.0 + sc_ref[...]) + sh_ref[...]).astype(BF16)
    r = _dot(h, w_ref[...])
    ql_ref[...] = _rms(r[:, :Q_LORA], gq_ref[...]).astype(BF16)
    ckv = _rms(r[:, Q_LORA:Q_LORA + KV_LORA], gkv_ref[...])
    t = r[:, Q_LORA + KV_LORA:] * cs_ref[...]
    kr = t[:, :QK_ROPE] + t[:, QK_ROPE:]
    kv_ref[:, :KV_LORA] = ckv
    kv_ref[:, KV_LORA:] = kr
    kvb_ref[:, :KV_LORA] = ckv.astype(BF16)
    lane = lax.broadcasted_iota(jnp.int32, (kr.shape[0], KV_PAD - KV_LORA), 1)
    tail = jnp.where(lane == QK_ROPE, 1.0, jnp.concatenate([kr, jnp.zeros_like(kr)], axis=-1))
    kvb_ref[:, KV_LORA:] = tail.astype(BF16)


def mla_proj(s, li, w_cat, g_q, g_kv, cs_tab, norm_pre):
    D, N = w_cat.shape
    out = (jax.ShapeDtypeStruct((s.G, s.R, Q_LORA), BF16),
           jax.ShapeDtypeStruct((s.G, s.R, KV_ROW), F32),
           jax.ShapeDtypeStruct((s.G, s.R, KV_PAD), BF16))
    return pl.pallas_call(
        _mla_proj_kernel,
        grid=(s.G, s.R // s.bm),
        in_specs=[s.rows(D), s.mod_spec(li, 3), s.mod_spec(li, 4), _gain_spec(li * 3 + 1, D),
                  pl.BlockSpec((D, N), lambda g, i: (0, 0)),
                  pl.BlockSpec((1, Q_LORA), lambda g, i: (0, 0)),
                  pl.BlockSpec((1, KV_LORA), lambda g, i: (0, 0)),
                  pl.BlockSpec((s.bm, LANE), lambda g, i: (i, 0))],
        out_specs=(s.rows(Q_LORA), s.rows(KV_ROW), s.rows(KV_PAD)),
        out_shape=out,
        compiler_params=_cparams("parallel", "parallel"),
        name="mla_proj",
    )(s.x, s.mod, s.mod, norm_pre, w_cat, g_q, g_kv, cs_tab)


def _q_expand_kernel(ql_ref, w_ref, wr_ref, cm_ref, sm_ref, q_ref, *, hb):
    ql = ql_ref[...]
    a = _dot(ql, w_ref[...])
    b = _dot(ql, wr_ref[...])
    cm, sm = cm_ref[...], sm_ref[...]
    for h in range(hb):
        sl = slice(h * QK_PAD, (h + 1) * QK_PAD)
        q_ref[:, sl] = (a[:, sl] * cm + b[:, sl] * sm).astype(BF16)


def q_expand(s, ql, w_q, w_qrot, cmap, smap):
    hb = 4
    nw = hb * QK_PAD
    wspec = pl.BlockSpec((Q_LORA, nw), lambda g, i, j: (0, j))
    tspec = pl.BlockSpec((s.bm, QK_PAD), lambda g, i, j: (i, 0))
    return pl.pallas_call(
        functools.partial(_q_expand_kernel, hb=hb),
        grid=(s.G, s.R // s.bm, MLA_HEADS // hb),
        in_specs=[s.rows(Q_LORA), wspec, wspec, tspec, tspec],
        out_specs=pl.BlockSpec((None, s.bm, nw), lambda g, i, j: (g, i, j)),
        out_shape=jax.ShapeDtypeStruct((s.G, s.R, MLA_HEADS * QK_PAD), BF16),
        compiler_params=_cparams("parallel", "parallel", "parallel"),
        name="q_expand",
    )(ql, w_q, w_qrot, cmap, smap)


def _kv_expand_kernel(kvb_ref, wk_ref, wv_ref, k_ref, v_ref):
    kvb = kvb_ref[...]
    k_ref[...] = _dot(kvb, wk_ref[...]).astype(BF16)
    v_ref[...] = _dot(kvb, wv_ref[...]).astype(BF16)


def kv_expand(s, kvb, w_k, w_v):
    hb = 4
    return pl.pallas_call(
        _kv_expand_kernel,
        grid=(s.G, s.R // s.bm, MLA_HEADS // hb),
        in_specs=[s.rows(KV_PAD),
                  pl.BlockSpec((KV_PAD, hb * QK_PAD), lambda g, i, j: (0, j)),
                  pl.BlockSpec((KV_PAD, hb * QK_PAD), lambda g, i, j: (0, j))],
        out_specs=(pl.BlockSpec((None, s.bm, hb * QK_PAD), lambda g, i, j: (g, i, j)),
                   pl.BlockSpec((None, s.bm, hb * QK_PAD), lambda g, i, j: (g, i, j))),
        out_shape=(jax.ShapeDtypeStruct((s.G, s.R, MLA_HEADS * QK_PAD), BF16),
                   jax.ShapeDtypeStruct((s.G, s.R, MLA_HEADS * QK_PAD), BF16)),
        compiler_params=_cparams("parallel", "parallel", "parallel"),
        name="kv_expand",
    )(kvb, w_k, w_v)


def _flash_kernel(q_ref, k_ref, v_ref, o_ref, m_ref, acc_ref, *, blk, hb):
    i = pl.program_id(2)
    m_ref[...] = jnp.full_like(m_ref, NEG_INF)
    acc_ref[...] = jnp.zeros_like(acc_ref)

    def block(j, diagonal):
        r0 = pl.multiple_of(j * blk, blk)
        for h in range(hb):
            hs = slice(h * QK_PAD, (h + 1) * QK_PAD)
            s = _dot_nt(q_ref[:, hs], k_ref[pl.ds(r0, blk), hs])
            if diagonal:
                row = lax.broadcasted_iota(jnp.int32, (blk, blk), 0)
                col = lax.broadcasted_iota(jnp.int32, (blk, blk), 1)
                s = jnp.where(col <= row, s, NEG_INF)
            m_prev = m_ref[h]
            m_new = jnp.maximum(m_prev, jnp.max(s, axis=-1, keepdims=True))
            alpha = jnp.exp(m_prev - m_new)
            p = jnp.exp(s - jnp.concatenate([m_new] * (blk // LANE), axis=1))
            pv = _dot(p.astype(BF16), v_ref[pl.ds(r0, blk), hs])
            acc_ref[h] = jnp.concatenate([alpha] * (QK_PAD // LANE), axis=1) * acc_ref[h] + pv
            m_ref[h] = m_new

    def below_diagonal(j, carry):
        block(j, False)
        return carry

    lax.fori_loop(0, i, below_diagonal, 0)
    block(i, True)
    for h in range(hb):
        acc = acc_ref[h]
        o_ref[:, h * V_HEAD:(h + 1) * V_HEAD] = (acc[:, :V_HEAD] / acc[:, V_HEAD:V_HEAD + 1]).astype(o_ref.dtype)


def flash_prompt(q, k, v):
    B, L, _ = q.shape
    blk = min(512, L)
    hb = 4
    seq = pl.BlockSpec((None, L, hb * QK_PAD), lambda b, h, i: (b, 0, h))
    return pl.pallas_call(
        functools.partial(_flash_kernel, blk=blk, hb=hb),
        grid=(B, MLA_HEADS // hb, L // blk),
        in_specs=[pl.BlockSpec((None, blk, hb * QK_PAD), lambda b, h, i: (b, i, h)), seq, seq],
        out_specs=pl.BlockSpec((None, blk, hb * V_HEAD), lambda b, h, i: (b, i, h)),
        out_shape=jax.ShapeDtypeStruct((B, L, MLA_HEADS * V_HEAD), BF16),
        scratch_shapes=[pltpu.VMEM((hb, blk, LANE), F32), pltpu.VMEM((hb, blk, QK_PAD), F32)],
        compiler_params=_cparams("parallel", "parallel", "parallel"),
        name="flash_prompt",
    )(q, k, v)


def _q_absorb_kernel(q_ref, w_ref, qc_ref):
    q = q_ref[...]
    qc_ref[:, :KV_LORA] = _dot(q[:, :QK_NOPE], w_ref[...]).astype(BF16)
    qc_ref[:, KV_LORA:] = q[:, QK_NOPE:]


def q_absorb(q, w_ukT):
    R = q.shape[0]
    return pl.pallas_call(
        _q_absorb_kernel,
        grid=(MLA_HEADS,),
        in_specs=[pl.BlockSpec((R, QK_PAD), lambda h: (0, h)),
                  pl.BlockSpec((None, QK_NOPE, KV_LORA), lambda h: (h, 0, 0))],
        out_specs=pl.BlockSpec((None, R, KV_PAD), lambda h: (h, 0, 0)),
        out_shape=jax.ShapeDtypeStruct((MLA_HEADS, R, KV_PAD), BF16),
        compiler_params=_cparams("parallel"),
        name="q_absorb",
    )(q, w_ukT)


PAGE_SLOTS = 3


def _paged_kernel(pt_ref, qc_ref, newt_ref, cache_ref, o_ref, buf_ref, sem_ref, kb_ref, kn_ref, m_ref, l_ref, acc_ref,
                  *, layer, pp, group, steps_per_seq, heads):
    t = pl.program_id(0)
    n_steps = pl.num_programs(0)
    j = t % steps_per_seq
    rows = qc_ref.shape[0]
    gw = group * PAGE_SIZE

    def page_copy(step, k):
        slot = step % PAGE_SLOTS
        page = pt_ref[step * pp + k]
        return pltpu.make_async_copy(cache_ref.at[layer, page], buf_ref.at[slot, k], sem_ref.at[slot])

    def fetch(step):
        for k in range(pp):
            page_copy(step, k).start()

    @pl.when(t == 0)
    def _():
        for ahead in range(PAGE_SLOTS - 1):
            @pl.when(ahead < n_steps)
            def _():
                fetch(ahead)

    @pl.when(t + (PAGE_SLOTS - 1) < n_steps)
    def _():
        fetch(t + (PAGE_SLOTS - 1))

    @pl.when(j == 0)
    def _():
        m_ref[...] = jnp.full_like(m_ref, NEG_INF)
        l_ref[...] = jnp.zeros_like(l_ref)
        acc_ref[...] = jnp.zeros_like(acc_ref)
        kb_ref[KV_ROW:, :] = jnp.zeros((KV_PAD - KV_ROW, kb_ref.shape[1]), BF16)

    def update(key_refs, visible):
        q = qc_ref[...]
        ss = [_dot(q, kr[...]) for kr in key_refs]
        if visible is not None:
            ss = [jnp.where(visible, s, NEG_INF) for s in ss]
        m_prev = m_ref[...]
        m_new = m_prev
        for s in ss:
            m_new = jnp.maximum(m_new, jnp.max(s, axis=-1, keepdims=True))
        alpha = jnp.exp(m_prev - m_new)
        l_new = alpha * l_ref[...]
        acc = alpha * acc_ref[...]
        for s, kr in zip(ss, key_refs):
            p = jnp.exp(s - m_new)
            l_new = l_new + jnp.sum(p, axis=-1, keepdims=True)
            acc = acc + _dot_nt(p.astype(BF16), kr[:KV_LORA, :])
        l_ref[...] = l_new
        acc_ref[...] = acc
        m_ref[...] = m_new

    slot = t % PAGE_SLOTS
    for k in range(pp):
        page_copy(t, k).wait()
    for k in range(pp):
        kb_ref[:KV_ROW, k * PAGE_SIZE:(k + 1) * PAGE_SIZE] = buf_ref[slot, k].astype(BF16)
    update([kb_ref.at[:, g * gw:(g + 1) * gw] for g in range(pp // group)], None)

    @pl.when(j == steps_per_seq - 1)
    def _():
        T = newt_ref.shape[1]
        kn_ref[...] = jnp.zeros_like(kn_ref)
        kn_ref[:KV_ROW, :T] = newt_ref[...].astype(BF16)
        row = lax.broadcasted_iota(jnp.int32, (rows, PAGE_SIZE), 0)
        col = lax.broadcasted_iota(jnp.int32, (rows, PAGE_SIZE), 1)
        update([kn_ref], col * heads <= row)
        o_ref[...] = acc_ref[...] / l_ref[...]


def paged_attention(qc, kv_new, cache_t, layer, page_table):
    Bd, rows, _ = qc.shape
    T = kv_new.shape[1]
    n_pages = page_table.shape[1]
    pp = 32 if n_pages % 32 == 0 else n_pages
    group = 8 if pp % 8 == 0 else pp
    sps = n_pages // pp
    grid_spec = pltpu.PrefetchScalarGridSpec(
        num_scalar_prefetch=1,
        grid=(Bd * sps,),
        in_specs=[pl.BlockSpec((None, rows, KV_PAD), lambda t, pt: (t // sps, 0, 0)),
                  pl.BlockSpec((None, KV_ROW, T), lambda t, pt: (t // sps, 0, 0)),
                  pl.BlockSpec(memory_space=pl.ANY)],
        out_specs=pl.BlockSpec((None, rows, KV_LORA), lambda t, pt: (t // sps, 0, 0)),
        scratch_shapes=[pltpu.VMEM((PAGE_SLOTS, pp, KV_ROW, PAGE_SIZE), F32), pltpu.SemaphoreType.DMA((PAGE_SLOTS,)),
                        pltpu.VMEM((KV_PAD, pp * PAGE_SIZE), BF16), pltpu.VMEM((KV_PAD, PAGE_SIZE), BF16),
                        pltpu.VMEM((rows, 1), F32), pltpu.VMEM((rows, 1), F32), pltpu.VMEM((rows, KV_LORA), F32)])
    return pl.pallas_call(
        functools.partial(_paged_kernel, layer=layer, pp=pp, group=group, steps_per_seq=sps, heads=rows // T),
        grid_spec=grid_spec,
        out_shape=jax.ShapeDtypeStruct((Bd, rows, KV_LORA), F32),
        compiler_params=_cparams("arbitrary"),
        name="paged_attention",
    )(page_table.reshape(-1), qc, kv_new.transpose(0, 2, 1), cache_t)


def _v_up_kernel(o_ref, w_ref, y_ref):
    y_ref[...] = _dot(o_ref[...].astype(BF16), w_ref[...]).astype(BF16)


def v_up(o_lat, w_uv):
    R = o_lat.shape[0]
    return pl.pallas_call(
        _v_up_kernel,
        grid=(MLA_HEADS,),
        in_specs=[pl.BlockSpec((R, KV_LORA), lambda h: (0, h)), pl.BlockSpec((KV_LORA, V_HEAD), lambda h: (0, h))],
        out_specs=pl.BlockSpec((R, V_HEAD), lambda h: (0, h)),
        out_shape=jax.ShapeDtypeStruct((R, MLA_HEADS * V_HEAD), BF16),
        compiler_params=_cparams("parallel"),
        name="v_up",
    )(o_lat, w_uv)


def _hg_proj_kernel(x_ref, sh_ref, sc_ref, gpre_ref, wq_ref, wf_ref, wi_ref, wg_ref, lb_ref,
                    q_ref, k_ref, v_ref, lf_ref, g_ref, h_ref):
    @pl.when(pl.program_id(2) == 0)
    def _():
        _prenorm_into(h_ref, x_ref, sh_ref, sc_ref, gpre_ref)

    h = h_ref[...]
    q_ref[...] = _silu(_dot(h, wq_ref[...])).astype(q_ref.dtype)
    fz = _dot(h, wf_ref[...])
    lb = lb_ref[...]
    a = jnp.log(lb)
    b = jnp.log1p(-lb) + (jnp.minimum(fz, 0.0) - jnp.log1p(jnp.exp(-jnp.abs(fz))))
    lf_ref[...] = jnp.maximum(a, b) + jnp.log1p(jnp.exp(-jnp.abs(a - b)))
    k_ref[...] = ((1.0 - lb) * _sigmoid(-fz)).astype(k_ref.dtype)
    v_ref[...] = _dot(h, wi_ref[...]).astype(v_ref.dtype)
    g_ref[...] = _silu(_dot(h, wg_ref[...])).astype(g_ref.dtype)


PROJ_BM = 1024
PROJ_BN = 256


def _blocked(w):
    D, N = w.shape
    return w.reshape(D, N // PROJ_BN, PROJ_BN).transpose(1, 0, 2).astype(BF16)


def hg_proj(s, li, w_q, w_f, w_i, w_g, lb, norm_pre):
    s = Stream(s.x, s.mod, s.per_row, bm=min(PROJ_BM, s.R))
    nb, D, bn = w_q.shape
    wspec = pl.BlockSpec((None, D, bn), lambda g, i, j: (j, 0, 0))
    ospec = pl.BlockSpec((None, s.bm, bn), lambda g, i, j: (g, i, j))
    shape = lambda dt: jax.ShapeDtypeStruct((s.G, s.R, nb * bn), dt)
    return pl.pallas_call(
        _hg_proj_kernel,
        grid=(s.G, s.R // s.bm, nb),
        in_specs=[s.rows(D), s.mod_spec(li, 3), s.mod_spec(li, 4), _gain_spec(li * 3 + 1, D),
                  wspec, wspec, wspec, wspec, pl.BlockSpec((1, bn), lambda g, i, j: (0, j))],
        out_specs=(ospec,) * 5,
        out_shape=(shape(BF16), shape(BF16), shape(BF16), shape(F32), shape(BF16)),
        scratch_shapes=[pltpu.VMEM((s.bm, D), BF16)],
        compiler_params=_cparams("parallel", "parallel", "arbitrary"),
        name="hg_proj",
    )(s.x, s.mod, s.mod, norm_pre, w_q, w_f, w_i, w_g, lb)


def _cumsum_rows(x, span):
    row = lax.broadcasted_iota(jnp.int32, x.shape, 0)
    pos = jnp.bitwise_and(row, span - 1)
    sh = 1
    while sh < span:
        x = x + jnp.where(pos >= sh, pltpu.roll(x, sh, 0), 0.0)
        sh *= 2
    return x


def _chunk_id(r):
    return lax.shift_right_logical(r, HG_CHUNK.bit_length() - 1)


def _hg_scan_kernel(q_ref, k_ref, v_ref, lf_ref, g_ref, s0_ref, gn_ref, o_ref, so_ref, st_ref, *, hb, chained):
    n = HG_BLOCK // HG_CHUNK
    first = pl.program_id(2) == 0
    last = pl.program_id(2) == pl.num_programs(2) - 1

    if chained:
        @pl.when(first)
        def _():
            for h in range(hb):
                st_ref[h] = s0_ref[h].T

    row = lax.broadcasted_iota(jnp.int32, (HG_BLOCK, HG_BLOCK), 0)
    col = lax.broadcasted_iota(jnp.int32, (HG_BLOCK, HG_BLOCK), 1)
    causal = (_chunk_id(row) == _chunk_id(col)) & (col <= row)
    chunk_of_row = _chunk_id(lax.broadcasted_iota(jnp.int32, (HG_BLOCK, HG_DK), 0))

    for h in range(hb):
        sl = slice(h * HG_DK, (h + 1) * HG_DK)
        b = _cumsum_rows(lf_ref[:, sl], HG_CHUNK)
        b_last = jnp.concatenate(
            [jnp.broadcast_to(b[c * HG_CHUNK + HG_CHUNK - 1:(c + 1) * HG_CHUNK, :], (HG_CHUNK, HG_DK)) for c in range(n)],
            axis=0)
        q, k, v = q_ref[:, sl].astype(F32), k_ref[:, sl].astype(F32), v_ref[:, sl].astype(F32)
        qg = q * jnp.exp(b)
        kd = (k * jnp.exp(-b)).astype(BF16)
        kl = k * jnp.exp(b_last - b)
        qgb = qg.astype(BF16)
        vb = v.astype(BF16)
        a = jnp.where(causal, _dot_nt(qgb, kd), 0.0)
        o = _dot(a.astype(BF16), vb)
        klx = jnp.concatenate([jnp.where(chunk_of_row == c, kl, 0.0).astype(BF16) for c in range(n)], axis=1)
        ut = _dot(v.T.astype(BF16), klx)
        sts = []
        if chained:
            st = st_ref[h]
        for c in range(n):
            if not chained:
                st = s0_ref[c, h].T
            sts.append(st.astype(BF16))
            dec = jnp.exp(b_last[c * HG_CHUNK:c * HG_CHUNK + 1, :])
            st = st * dec + ut[:, c * HG_DK:(c + 1) * HG_DK]
            if not chained:
                so_ref[c, h] = st.T
        if chained:
            st_ref[h] = st

            @pl.when(last)
            def _():
                so_ref[h] = st.T
        qx = jnp.concatenate([jnp.where(chunk_of_row == c, qg, 0.0).astype(BF16) for c in range(n)], axis=1)
        o = o + _dot_nt(qx, jnp.concatenate(sts, axis=1))
        o_ref[:, sl] = (_rms(o, gn_ref[...]) * g_ref[:, sl]).astype(BF16)


def hg_scan(q, k, v, lf, gate, s0, g_norm, chained):
    G, R, N = q.shape
    hb = 4 if chained else 2
    n = HG_BLOCK // HG_CHUNK
    rows = pl.BlockSpec((None, HG_BLOCK, hb * HG_DK), lambda g, h, l: (g, l, h))
    if chained:
        sspec = pl.BlockSpec((None, hb, HG_DK, HG_DV), lambda g, h, l: (g, h, 0, 0))
    else:
        sspec = pl.BlockSpec((None, n, hb, HG_DK, HG_DV), lambda g, h, l: (g, 0, h, 0, 0))
    return pl.pallas_call(
        functools.partial(_hg_scan_kernel, hb=hb, chained=chained),
        grid=(G, HG_HEADS // hb, R // HG_BLOCK),
        in_specs=[rows, rows, rows, rows, rows, sspec, pl.BlockSpec((1, HG_DV), lambda g, h, l: (0, 0))],
        out_specs=(rows, sspec),
        out_shape=(jax.ShapeDtypeStruct((G, R, N), BF16), jax.ShapeDtypeStruct(s0.shape, F32)),
        scratch_shapes=[pltpu.VMEM((hb, HG_DV, HG_DK), F32)],
        compiler_params=_cparams("parallel", "parallel", "arbitrary"),
        name="hg_scan",
    )(q, k, v, lf, gate, s0, g_norm)


def _lru_proj_kernel(x_ref, sh_ref, sc_ref, gpre_ref, wx_ref, wy_ref, u_ref, g_ref, h_ref):
    @pl.when(pl.program_id(2) == 0)
    def _():
        _prenorm_into(h_ref, x_ref, sh_ref, sc_ref, gpre_ref)

    h = h_ref[...]
    u_ref[...] = _dot(h, wx_ref[...])
    g_ref[...] = _gelu_tanh(_dot(h, wy_ref[...])).astype(g_ref.dtype)


def lru_proj(s, li, w_x, w_y, norm_pre):
    s = Stream(s.x, s.mod, s.per_row, bm=min(PROJ_BM, s.R))
    nb, D, bn = w_x.shape
    wspec = pl.BlockSpec((None, D, bn), lambda g, i, j: (j, 0, 0))
    ospec = pl.BlockSpec((None, s.bm, bn), lambda g, i, j: (g, i, j))
    return pl.pallas_call(
        _lru_proj_kernel,
        grid=(s.G, s.R // s.bm, nb),
        in_specs=[s.rows(D), s.mod_spec(li, 3), s.mod_spec(li, 4), _gain_spec(li * 3 + 1, D), wspec, wspec],
        out_specs=(ospec, ospec),
        out_shape=(jax.ShapeDtypeStruct((s.G, s.R, nb * bn), F32), jax.ShapeDtypeStruct((s.G, s.R, nb * bn), BF16)),
        scratch_shapes=[pltpu.VMEM((s.bm, D), BF16)],
        compiler_params=_cparams("parallel", "parallel", "arbitrary"),
        name="lru_proj",
    )(s.x, s.mod, s.mod, norm_pre, w_x, w_y)


def _lru_gates(xc, n, wra_ref, bra_ref, wix_ref, bix_ref, lam_ref):
    sl = slice(n * LRU_BW, (n + 1) * LRU_BW)
    xb = xc.astype(BF16)
    r = _sigmoid(_dot(xb, wra_ref[n]) + bra_ref[:, sl])
    ig = _sigmoid(_dot(xb, wix_ref[n]) + bix_ref[:, sl])
    log_a = -LRU_C * r * _softplus(-lam_ref[:, sl])
    a = jnp.exp(log_a)
    return a, jnp.sqrt(-jnp.tanh(log_a) * (a * a + 1.0)) * (ig * xc)


def _lru_scan_kernel(u_ref, gate_ref, cb_ref, h0_ref, cw_ref, cbias_ref, wra_ref, bra_ref, wix_ref, bix_ref, lam_ref,
                     yg_ref, hl_ref, cbo_ref, ext_ref, a_ref, b_ref, hc_ref, *, bl):
    l = pl.program_id(1)
    keep = CONV_W - 1
    base = 8

    @pl.when(l == 0)
    def _():
        ext_ref[base - keep:base, :] = cb_ref[...]
        hc_ref[...] = h0_ref[...]

    ext_ref[base:base + bl, :] = u_ref[...]
    for n in range(LRU_BLOCKS):
        sl = slice(n * LRU_BW, (n + 1) * LRU_BW)
        xc = cbias_ref[:, sl]
        for j in range(CONV_W):
            xc = xc + ext_ref[base - keep + j:base - keep + j + bl, sl] * cw_ref[j:j + 1, sl]
        a, b = _lru_gates(xc, n, wra_ref, bra_ref, wix_ref, bix_ref, lam_ref)
        a_ref[:, sl] = a
        b_ref[:, sl] = b

    pos = lax.broadcasted_iota(jnp.int32, (8, a_ref.shape[1]), 0)

    def tile(t, h_prev):
        r0 = pl.multiple_of(t * 8, 8)
        a_t = a_ref[pl.ds(r0, 8), :]
        b_t = b_ref[pl.ds(r0, 8), :]
        for sh in (1, 2, 4):
            m = pos >= sh
            b_t = jnp.where(m, a_t * pltpu.roll(b_t, sh, 0) + b_t, b_t)
            a_t = jnp.where(m, a_t * pltpu.roll(a_t, sh, 0), a_t)
        h_t = b_t + a_t * h_prev
        b_ref[pl.ds(r0, 8), :] = h_t
        return h_t[7:8, :]

    h_last = lax.fori_loop(0, bl // 8, tile, hc_ref[...])
    hc_ref[...] = h_last
    yg_ref[...] = (b_ref[...] * gate_ref[...]).astype(BF16)
    tail = ext_ref[base + bl - keep:base + bl, :]
    ext_ref[base - keep:base, :] = tail

    @pl.when(l == pl.num_programs(1) - 1)
    def _():
        hl_ref[...] = h_last
        cbo_ref[...] = tail


def lru_scan(u, gate, conv_buf, h0, conv_w, conv_b, w_ra, b_ra, w_ix, b_ix, lam):
    B, L, W = u.shape
    bl = min(256, L)
    rows = pl.BlockSpec((None, bl, W), lambda b, l: (b, l, 0))
    vec = pl.BlockSpec((1, W), lambda b, l: (0, 0))
    wblk = pl.BlockSpec((LRU_BLOCKS, LRU_BW, LRU_BW), lambda b, l: (0, 0, 0))
    hspec = pl.BlockSpec((None, 1, W), lambda b, l: (b, 0, 0))
    cspec = pl.BlockSpec((None, CONV_W - 1, W), lambda b, l: (b, 0, 0))
    return pl.pallas_call(
        functools.partial(_lru_scan_kernel, bl=bl),
        grid=(B, L // bl),
        in_specs=[rows, rows, cspec, hspec, pl.BlockSpec((CONV_W, W), lambda b, l: (0, 0)), vec,
                  wblk, vec, wblk, vec, vec],
        out_specs=(rows, hspec, cspec),
        out_shape=(jax.ShapeDtypeStruct((B, L, W), BF16), jax.ShapeDtypeStruct((B, 1, W), F32),
                   jax.ShapeDtypeStruct((B, CONV_W - 1, W), F32)),
        scratch_shapes=[pltpu.VMEM((bl + 8, W), F32), pltpu.VMEM((bl, W), F32), pltpu.VMEM((bl, W), F32),
                        pltpu.VMEM((1, W), F32)],
        compiler_params=_cparams("parallel", "arbitrary"),
        name="lru_scan",
    )(u, gate, conv_buf, h0.reshape(B, 1, W), conv_w, conv_b, w_ra, b_ra, w_ix, b_ix, lam)


def _lru_step_kernel(u_ref, gate_ref, cb_ref, h0_ref, cw_ref, cbias_ref, wra_ref, bra_ref, wix_ref, bix_ref, lam_ref,
                     yg_ref, hl_ref, cbo_ref):
    T = u_ref.shape[0]
    keep = CONV_W - 1
    for n in range(LRU_BLOCKS):
        sl = slice(n * LRU_BW, (n + 1) * LRU_BW)
        ext = [cb_ref[j, :, sl] for j in range(keep)] + [u_ref[t, :, sl] for t in range(T)]
        h = h0_ref[:, sl]
        for t in range(T):
            xc = cbias_ref[:, sl]
            for j in range(CONV_W):
                xc = xc + ext[t + j] * cw_ref[j:j + 1, sl]
            a, b = _lru_gates(xc, n, wra_ref, bra_ref, wix_ref, bix_ref, lam_ref)
            h = a * h + b
            yg_ref[t, :, sl] = (h * gate_ref[t, :, sl]).astype(BF16)
        hl_ref[:, sl] = h
        for j in range(keep):
            cbo_ref[j, :, sl] = ext[T + j]


def lru_step(u, gate, conv_buf, h0, conv_w, conv_b, w_ra, b_ra, w_ix, b_ix, lam):
    T, Bd, W = u.shape
    full = lambda *shape: pl.BlockSpec(shape, lambda i: (0,) * len(shape))
    return pl.pallas_call(
        _lru_step_kernel,
        grid=(1,),
        in_specs=[full(T, Bd, W), full(T, Bd, W), full(CONV_W - 1, Bd, W), full(Bd, W), full(CONV_W, W), full(1, W),
                  full(LRU_BLOCKS, LRU_BW, LRU_BW), full(1, W), full(LRU_BLOCKS, LRU_BW, LRU_BW), full(1, W), full(1, W)],
        out_specs=(full(T, Bd, W), full(Bd, W), full(CONV_W - 1, Bd, W)),
        out_shape=(jax.ShapeDtypeStruct((T, Bd, W), BF16), jax.ShapeDtypeStruct((Bd, W), F32),
                   jax.ShapeDtypeStruct((CONV_W - 1, Bd, W), F32)),
        compiler_params=_cparams("arbitrary"),
        name="lru_step",
    )(u, gate, conv_buf, h0, conv_w, conv_b, w_ra, b_ra, w_ix, b_ix, lam)


def _rope_tables(pos):
    half = QK_ROPE // 2
    inv = ROPE_THETA ** (-jnp.arange(half, dtype=F32) / half)
    ang = pos.astype(F32)[:, None] * inv[None, :]
    cos, sin = jnp.cos(ang), jnp.sin(ang)
    cos2, sin2 = jnp.concatenate([cos, cos], -1), jnp.concatenate([sin, sin], -1)
    ones, zeros = jnp.ones((pos.shape[0], QK_NOPE), F32), jnp.zeros((pos.shape[0], QK_NOPE), F32)
    pad = jnp.zeros((pos.shape[0], QK_PAD - QK_NOPE - QK_ROPE), F32)
    cs_tab = jnp.concatenate([cos2, sin2], -1)
    cmap = jnp.concatenate([ones, cos2, pad], -1) * MLA_SCALE
    smap = jnp.concatenate([zeros, sin2, pad], -1) * MLA_SCALE
    return cs_tab, cmap, smap


def _rot_cols(w):
    half = w.shape[-1] // 2
    return jnp.concatenate([-w[..., half:], w[..., :half]], -1)


def _mla_weights(w_dq, w_uq, w_dkv, w_uk, w_uv):
    kr = w_dkv[:, KV_LORA:]
    w_cat = jnp.concatenate([w_dq, w_dkv, _rot_cols(kr)], axis=1).astype(BF16)
    uq = w_uq.reshape(Q_LORA, MLA_HEADS, QK_NOPE + QK_ROPE)
    nope, rope = uq[..., :QK_NOPE], uq[..., QK_NOPE:]
    pad = jnp.zeros((Q_LORA, MLA_HEADS, QK_PAD - QK_NOPE - QK_ROPE), F32)
    w_q = jnp.concatenate([nope, rope, pad], -1).reshape(Q_LORA, MLA_HEADS * QK_PAD).astype(BF16)
    w_qrot = jnp.concatenate([jnp.zeros_like(nope), _rot_cols(rope), pad], -1)
    w_qrot = w_qrot.reshape(Q_LORA, MLA_HEADS * QK_PAD).astype(BF16)
    w_k = jnp.zeros((KV_PAD, MLA_HEADS, QK_PAD), F32)
    w_k = w_k.at[:KV_LORA, :, :QK_NOPE].set(w_uk)
    eye = jnp.broadcast_to(jnp.eye(QK_ROPE, dtype=F32)[:, None, :], (QK_ROPE, MLA_HEADS, QK_ROPE))
    w_k = w_k.at[KV_LORA:KV_ROW, :, QK_NOPE:QK_NOPE + QK_ROPE].set(eye)
    w_k = w_k.reshape(KV_PAD, MLA_HEADS * QK_PAD).astype(BF16)
    w_v = jnp.zeros((KV_PAD, MLA_HEADS, QK_PAD), F32)
    w_v = w_v.at[:KV_LORA, :, :V_HEAD].set(w_uv).at[KV_ROW, :, V_HEAD].set(1.0)
    w_v = w_v.reshape(KV_PAD, MLA_HEADS * QK_PAD).astype(BF16)
    w_ukT = w_uk.transpose(1, 2, 0).astype(BF16)
    w_uv2 = w_uv.reshape(KV_LORA, MLA_HEADS * V_HEAD).astype(BF16)
    return w_cat, w_q, w_qrot, w_k, w_v, w_ukT, w_uv2


def _hg_lower_bounds(lb_param):
    cs = jnp.cumsum(jax.nn.softmax(lb_param.astype(F32), axis=0), axis=0)
    return cs - cs[0]


def kernel(x_prompt, x_sample, c_prompt, c_sample, cache_mla_kv, page_table, state_hgrn, state_lru_h, state_lru_conv, ada_w, ada_b, norm_pre, norm_post, ffn1_wg, ffn1_wu, ffn1_wd, ffn2_wg, ffn2_wu, ffn2_wd, mla_w_dq, mla_g_q, mla_w_uq, mla_w_dkv, mla_g_kv, mla_w_uk, mla_w_uv, mla_w_o, hg_lb, hg_w_q, hg_w_f, hg_w_i, hg_w_g, hg_g_norm, hg_w_o, lru_w_x, lru_w_y, lru_conv_w, lru_conv_b, lru_w_ra, lru_b_ra, lru_w_ix, lru_b_ix, lru_lam, lru_w_o):
    B, L, D = x_prompt.shape
    Bd, T, _ = x_sample.shape
    n_pages = page_table.shape[1]
    past_len = n_pages * PAGE_SIZE

    n_s = Bd * T
    c_all = jnp.concatenate([jnp.repeat(c_sample, T, axis=0), c_prompt, jnp.zeros((-(n_s + B) % 8, D), F32)], axis=0)
    mod = ada_mod(c_all, ada_w, ada_b)
    mod_p = mod[:, n_s:n_s + B].reshape(DEPTH * B, 1, N_MOD * D)
    sp = Stream(x_prompt, mod_p, per_row=False)
    ss = Stream(x_sample.reshape(1, n_s, D), mod, per_row=True)

    npre = norm_pre.reshape(DEPTH * 3, 1, D)
    npost = norm_post.reshape(DEPTH * 3, 1, D)
    bf = lambda w: w.astype(BF16)
    f1 = (ffn1_wg, ffn1_wu, ffn1_wd)
    f2 = (ffn2_wg, ffn2_wu, ffn2_wd)
    cache_t = jnp.swapaxes(cache_mla_kv, 2, 3)

    tab_p = _rope_tables(jnp.arange(L))
    tab_s = _rope_tables(jnp.tile(past_len + jnp.arange(T), Bd))
    lbs = _hg_lower_bounds(hg_lb)

    kv_p, kv_s, hg_p, hg_s, lh_p, lh_s, lc_p, lc_s = [], [], [], [], [], [], [], []
    for li in range(DEPTH):
        xs, w_bf = ffn(ss, li, 0, *f1, npre, npost, emit=True)
        ss = ss.with_x(xs)
        sp = sp.with_x(ffn(sp, li, 0, *w_bf, npre, npost, emit=False))
        kind, j = li % N_MIXERS, li // N_MIXERS
        if kind == 0:
            w_cat, w_q, w_qrot, w_k, w_v, w_ukT, w_uv2 = _mla_weights(
                mla_w_dq[j], mla_w_uq[j], mla_w_dkv[j], mla_w_uk[j], mla_w_uv[j])
            g_q, g_kv = mla_g_q[j].reshape(1, Q_LORA), mla_g_kv[j].reshape(1, KV_LORA)
            w_o = bf(mla_w_o[j])
            ql, kvr, kvb = mla_proj(sp, li, w_cat, g_q, g_kv, tab_p[0], npre)
            q = q_expand(sp, ql, w_q, w_qrot, tab_p[1], tab_p[2])
            kk, vv = kv_expand(sp, kvb, w_k, w_v)
            o = flash_prompt(q, kk, vv)
            sp = sp.with_x(mm_post(sp, li, o, w_o, npost))
            kv_p.append(kvr)
            ql, kvr, _ = mla_proj(ss, li, w_cat, g_q, g_kv, tab_s[0], npre)
            q = q_expand(ss, ql, w_q, w_qrot, tab_s[1], tab_s[2])
            qc = q_absorb(q[0], w_ukT)
            qc = qc.reshape(MLA_HEADS, Bd, T, KV_PAD).transpose(1, 2, 0, 3).reshape(Bd, T * MLA_HEADS, KV_PAD)
            kv_new = kvr.reshape(Bd, T, KV_ROW)
            o_lat = paged_attention(qc, kv_new, cache_t, j, page_table)
            o = v_up(o_lat.reshape(Bd * T, MLA_HEADS * KV_LORA), w_uv2)
            ss = ss.with_x(mm_post(ss, li, o.reshape(1, Bd * T, -1), w_o, npost))
            kv_s.append(kv_new)
        elif kind == 1:
            lb = lbs[li].reshape(1, -1)
            ws = (_blocked(hg_w_q[j]), _blocked(hg_w_f[j]), _blocked(hg_w_i[j]), _blocked(hg_w_g[j]))
            g_n = hg_g_norm[j].reshape(1, HG_DV)
            w_o = bf(hg_w_o[j])
            q, k, v, lf, gate = hg_proj(sp, li, *ws, lb, npre)
            s0 = jnp.zeros((B, HG_HEADS, HG_DK, HG_DV), F32)
            o, s_new = hg_scan(q, k, v, lf, gate, s0, g_n, chained=True)
            sp = sp.with_x(mm_post(sp, li, o, w_o, npost))
            hg_p.append(s_new)
            n_seq = HG_BLOCK // HG_CHUNK
            def to_chunks(a):
                a = jnp.pad(a.reshape(Bd, T, -1), ((0, 0), (HG_CHUNK - T, 0), (0, 0)))
                return a.reshape(Bd // n_seq, HG_BLOCK, -1)
            qs, ks, vs, lfs, gs = [to_chunks(a) for a in hg_proj(ss, li, *ws, lb, npre)]
            s0 = state_hgrn[j].reshape(Bd // n_seq, n_seq, HG_HEADS, HG_DK, HG_DV)
            o, s_new = hg_scan(qs, ks, vs, lfs, gs, s0, g_n, chained=False)
            o = o.reshape(Bd, HG_CHUNK, -1)[:, HG_CHUNK - T:].reshape(1, Bd * T, -1)
            ss = ss.with_x(mm_post(ss, li, o, w_o, npost))
            hg_s.append(s_new.reshape(Bd, HG_HEADS, HG_DK, HG_DV))
        else:
            w_x, w_y, w_o = _blocked(lru_w_x[j]), _blocked(lru_w_y[j]), bf(lru_w_o[j])
            W = w_o.shape[0]
            prm = (lru_conv_w[j], lru_conv_b[j].reshape(1, W), bf(lru_w_ra[j]), lru_b_ra[j].reshape(1, W),
                   bf(lru_w_ix[j]), lru_b_ix[j].reshape(1, W), lru_lam[j].reshape(1, W))
            u, gate = lru_proj(sp, li, w_x, w_y, npre)
            yg, h_last, cb = lru_scan(u, gate, jnp.zeros((B, CONV_W - 1, W), F32), jnp.zeros((B, W), F32), *prm)
            sp = sp.with_x(mm_post(sp, li, yg, w_o, npost))
            lh_p.append(h_last.reshape(B, W))
            lc_p.append(cb)
            u, gate = lru_proj(ss, li, w_x, w_y, npre)
            tm = lambda a: a.reshape(Bd, T, W).transpose(1, 0, 2)
            yg, h_last, cb = lru_step(tm(u), tm(gate), state_lru_conv[j].transpose(1, 0, 2), state_lru_h[j], *prm)
            ss = ss.with_x(mm_post(ss, li, yg.transpose(1, 0, 2).reshape(1, Bd * T, W), w_o, npost))
            lh_s.append(h_last)
            lc_s.append(cb.transpose(1, 0, 2))
        xs, w_bf = ffn(ss, li, 2, *f2, npre, npost, emit=True)
        ss = ss.with_x(xs)
        sp = sp.with_x(ffn(sp, li, 2, *w_bf, npre, npost, emit=False))

    return (sp.x, ss.x.reshape(Bd, T, D), jnp.stack(kv_p), jnp.stack(kv_s), jnp.stack(hg_p), jnp.stack(hg_s),
            jnp.stack(lh_p), jnp.stack(lh_s), jnp.stack(lc_p), jnp.stack(lc_s))
```

```python
import functools

import jax
import jax.numpy as jnp
from jax import lax
from jax.experimental import pallas as pl
from jax.experimental.pallas import tpu as pltpu

F32 = jnp.float32
BF16 = jnp.bfloat16

DEPTH = 4
N_MIXERS = 3
NORM_EPS = 1e-6
N_MOD = 9
FFN_RES = 0.5

MLA_HEADS = 16
Q_LORA = 512
KV_LORA = 512
QK_NOPE = 128
QK_ROPE = 64
V_HEAD = 128
KV_ROW = KV_LORA + QK_ROPE
KV_PAD = 640
QK_PAD = 256
MLA_SCALE = (QK_NOPE + QK_ROPE) ** -0.5
ROPE_THETA = 10000.0
PAGE_SIZE = 128
NEG_INF = -1e30

HG_HEADS = 16
HG_DK = 128
HG_DV = 128
HG_CHUNK = 16
HG_BLOCK = 128

LRU_BLOCKS = 16
LRU_BW = 128
CONV_W = 4
LRU_C = 8.0

LANE = 128
VMEM_LIMIT = 56 * 1024 * 1024


def _cparams(*sem):
    return pltpu.CompilerParams(dimension_semantics=sem, vmem_limit_bytes=VMEM_LIMIT)


def _dot(a, b):
    return jnp.dot(a, b, preferred_element_type=F32)


def _dot_nt(a, b):
    return lax.dot_general(a, b, (((1,), (1,)), ((), ())), preferred_element_type=F32)


def _rms(x, g):
    ms = jnp.mean(x * x, axis=-1, keepdims=True)
    return x * lax.rsqrt(ms + NORM_EPS) * g


def _sigmoid(x):
    return 1.0 / (1.0 + jnp.exp(-x))


def _silu(x):
    return x * _sigmoid(x)


def _softplus(x):
    return jnp.maximum(x, 0.0) + jnp.log1p(jnp.exp(-jnp.abs(x)))


def _gelu_tanh(x):
    c = 0.7978845608028654
    return 0.5 * x * (1.0 + jnp.tanh(c * (x + 0.044715 * (x * x * x))))


ROW_CHUNK = 16
ROW_UNROLL = 4


def _row_chunks(n_rows, body):
    def step(c, carry):
        body(pl.ds(pl.multiple_of(c * ROW_CHUNK, ROW_CHUNK), ROW_CHUNK))
        return carry
    lax.fori_loop(0, n_rows // ROW_CHUNK, step, 0, unroll=ROW_UNROLL)


def _mod_rows(ref, rows):
    return ref[...] if ref.shape[0] == 1 else ref[rows, :]


def _prenorm_rows(h_ref, x_ref, sh_ref, sc_ref, g_ref, rows):
    y = _rms(x_ref[rows, :], g_ref[...])
    h_ref[rows, :] = (y * (1.0 + _mod_rows(sc_ref, rows)) + _mod_rows(sh_ref, rows)).astype(BF16)


def _prenorm_into(h_ref, x_ref, sh_ref, sc_ref, g_ref):
    _row_chunks(x_ref.shape[0], functools.partial(_prenorm_rows, h_ref, x_ref, sh_ref, sc_ref, g_ref))


def _postnorm_into(o_ref, y_ref, x_ref, gt_ref, g_ref, coef, clear_y=False):
    def body(rows):
        o_ref[rows, :] = x_ref[rows, :] + (coef * _mod_rows(gt_ref, rows)) * _rms(y_ref[rows, :], g_ref[...])
        if clear_y:
            y_ref[rows, :] = jnp.zeros((ROW_CHUNK, y_ref.shape[1]), y_ref.dtype)
    _row_chunks(x_ref.shape[0], body)


class Stream:
    def __init__(self, x, mod, per_row, bm=None):
        self.x, self.mod, self.per_row = x, mod, per_row
        self.G, self.R, self.D = x.shape
        self.bm = min(512, self.R) if bm is None else bm

    def with_x(self, x):
        return Stream(x, self.mod, self.per_row)

    def rows(self, width):
        return pl.BlockSpec((None, self.bm, width), lambda g, i, *_: (g, i, 0))

    def mod_spec(self, li, k):
        G, per_row = self.G, self.per_row
        rm = self.bm if per_row else 1
        return pl.BlockSpec((None, rm, self.D), lambda g, i, *_: (li * G + g, i if per_row else 0, k))


def _gain_spec(idx, width):
    return pl.BlockSpec((None, 1, width), lambda *_: (idx, 0, 0))


def _ada_kernel(c_ref, w_ref, b_ref, o_ref):
    cs = _silu(c_ref[...]).astype(BF16)
    o_ref[...] = _dot(cs, w_ref[...].astype(BF16)) + b_ref[...]


def ada_mod(c_all, ada_w, ada_b):
    Bc, D = c_all.shape
    N = ada_w.shape[-1]
    bn = 1024
    return pl.pallas_call(
        _ada_kernel,
        grid=(DEPTH, N // bn),
        in_specs=[pl.BlockSpec((Bc, D), lambda l, j: (0, 0)),
                  pl.BlockSpec((None, D, bn), lambda l, j: (l, 0, j)),
                  pl.BlockSpec((None, 1, bn), lambda l, j: (l, 0, j))],
        out_specs=pl.BlockSpec((None, Bc, bn), lambda l, j: (l, 0, j)),
        out_shape=jax.ShapeDtypeStruct((DEPTH, Bc, N), F32),
        compiler_params=_cparams("parallel", "parallel"),
        name="ada_mod",
    )(c_all, ada_w, ada_b.reshape(DEPTH, 1, N))


FFN_BF = 256
FFN_BM = 1024


def _swiglu_step(h, wg, wu, wd):
    g = _dot(h, wg)
    u = _dot(h, wu)
    return _dot((_silu(g) * u).astype(BF16), wd)


def _ffn_emit_kernel(x_ref, sh_ref, sc_ref, gt_ref, gpre_ref, gpost_ref, wg_ref, wu_ref, wd_ref,
                     o_ref, wgo_ref, wuo_ref, wdo_ref, h_ref, acc_ref):
    j = pl.program_id(0)

    @pl.when(j == 0)
    def _():
        _prenorm_into(h_ref, x_ref, sh_ref, sc_ref, gpre_ref)
        acc_ref[...] = jnp.zeros_like(acc_ref)

    wg, wu, wd = wg_ref[...].astype(BF16), wu_ref[...].astype(BF16), wd_ref[...].astype(BF16)
    wgo_ref[...] = wg
    wuo_ref[...] = wu
    wdo_ref[...] = wd
    acc_ref[...] += _swiglu_step(h_ref[...], wg, wu, wd)

    @pl.when(j == pl.num_programs(0) - 1)
    def _():
        _postnorm_into(o_ref, acc_ref, x_ref, gt_ref, gpost_ref, FFN_RES)


def ffn_emit(s, li, sub, wg, wu, wd, norm_pre, norm_post):
    assert s.G == 1 and s.R == s.bm
    D, DF = s.D, wd.shape[-2]
    k0, bf = 3 * sub, FFN_BF
    nj = DF // bf
    row = lambda width: pl.BlockSpec((None, s.bm, width), lambda j: (0, 0, 0))
    mod = lambda k: pl.BlockSpec((None, s.bm, D), lambda j: (li, 0, k))
    out = pl.pallas_call(
        _ffn_emit_kernel,
        grid=(nj,),
        in_specs=[row(D), mod(k0), mod(k0 + 1), mod(k0 + 2), _gain_spec(li * 3 + sub, D), _gain_spec(li * 3 + sub, D),
                  pl.BlockSpec((None, D, bf), lambda j: (li, 0, j)), pl.BlockSpec((None, D, bf), lambda j: (li, 0, j)),
                  pl.BlockSpec((None, bf, D), lambda j: (li, j, 0))],
        out_specs=[row(D), pl.BlockSpec((None, D, bf), lambda j: (j, 0, 0)), pl.BlockSpec((None, D, bf), lambda j: (j, 0, 0)),
                   pl.BlockSpec((bf, D), lambda j: (j, 0))],
        out_shape=[jax.ShapeDtypeStruct(s.x.shape, F32), jax.ShapeDtypeStruct((nj, D, bf), BF16),
                   jax.ShapeDtypeStruct((nj, D, bf), BF16), jax.ShapeDtypeStruct((DF, D), BF16)],
        scratch_shapes=[pltpu.VMEM((s.bm, D), BF16), pltpu.VMEM((s.bm, D), F32)],
        compiler_params=_cparams("arbitrary"),
        name="ffn_emit",
    )(s.x, s.mod, s.mod, s.mod, norm_pre, norm_post, wg, wu, wd)
    return out[0], out[1:]


def _ffn_kernel(x_ref, sh_ref, sc_ref, gt_ref, gpre_ref, gpost_ref, wg_ref, wu_ref, wd_ref, o_ref, h_ref, acc_ref):
    j = pl.program_id(2)
    first_tile = (pl.program_id(0) == 0) & (pl.program_id(1) == 0)

    @pl.when(j == 0)
    def _():
        _prenorm_into(h_ref, x_ref, sh_ref, sc_ref, gpre_ref)

        @pl.when(first_tile)
        def _():
            acc_ref[...] = jnp.zeros_like(acc_ref)

    acc_ref[...] += _swiglu_step(h_ref[...], wg_ref[...], wu_ref[...], wd_ref[...])

    @pl.when(j == pl.num_programs(2) - 1)
    def _():
        _postnorm_into(o_ref, acc_ref, x_ref, gt_ref, gpost_ref, FFN_RES, clear_y=True)


def ffn(s, li, sub, wg, wu, wd, norm_pre, norm_post):
    assert not s.per_row
    G, D = s.G, s.D
    DF = wd.shape[0]
    k0, bf = 3 * sub, FFN_BF
    bm = min(FFN_BM, s.R)
    rows = pl.BlockSpec((None, bm, D), lambda g, i, j: (g, i, 0))
    mod = lambda k: pl.BlockSpec((None, 1, D), lambda g, i, j: (li * G + g, 0, k))
    return pl.pallas_call(
        _ffn_kernel,
        grid=(G, s.R // bm, DF // bf),
        in_specs=[rows, mod(k0), mod(k0 + 1), mod(k0 + 2), _gain_spec(li * 3 + sub, D), _gain_spec(li * 3 + sub, D),
                  pl.BlockSpec((None, D, bf), lambda g, i, j: (j, 0, 0)), pl.BlockSpec((None, D, bf), lambda g, i, j: (j, 0, 0)),
                  pl.BlockSpec((bf, D), lambda g, i, j: (j, 0))],
        out_specs=rows,
        out_shape=jax.ShapeDtypeStruct(s.x.shape, F32),
        scratch_shapes=[pltpu.VMEM((bm, D), BF16), pltpu.VMEM((bm, D), F32)],
        compiler_params=_cparams("arbitrary", "arbitrary", "arbitrary"),
        name="ffn",
    )(s.x, s.mod, s.mod, s.mod, norm_pre, norm_post, wg, wu, wd)


def _mm_post_kernel(a_ref, w_ref, x_ref, gt_ref, gpost_ref, o_ref):
    y = _dot(a_ref[...], w_ref[...])
    o_ref[...] = x_ref[...] + gt_ref[...] * _rms(y, gpost_ref[...])


def mm_post(s, li, a, w, norm_post):
    K, D = w.shape
    bm = min(256, s.R)
    rows = lambda width: pl.BlockSpec((None, bm, width), lambda g, i: (g, i, 0))
    G, per_row = s.G, s.per_row
    gate = pl.BlockSpec((None, bm if per_row else 1, D), lambda g, i: (li * G + g, i if per_row else 0, 5))
    return pl.pallas_call(
        _mm_post_kernel,
        grid=(s.G, s.R // bm),
        in_specs=[rows(K), pl.BlockSpec((K, D), lambda g, i: (0, 0)), rows(D), gate, _gain_spec(li * 3 + 1, D)],
        out_specs=rows(D),
        out_shape=jax.ShapeDtypeStruct(s.x.shape, F32),
        compiler_params=_cparams("parallel", "parallel"),
        name="mm_post",
    )(a, w, s.x, s.mod, norm_post)


def _mla_proj_kernel(x_ref, sh_ref, sc_ref, gpre_ref, w_ref, gq_ref, gkv_ref, cs_ref, ql_ref, kv_ref, kvb_ref):
    h = (_rms(x_ref[...], gpre_ref[...]) * (1.0 + sc_ref[...]) + sh_ref[...]).astype(BF16)
    r = _dot(h, w_ref[...])
    ql_ref[...] = _rms(r[:, :Q_LORA], gq_ref[...]).astype(BF16)
    ckv = _rms(r[:, Q_LORA:Q_LORA + KV_LORA], gkv_ref[...])
    t = r[:, Q_LORA + KV_LORA:] * cs_ref[...]
    kr = t[:, :QK_ROPE] + t[:, QK_ROPE:]
    kv_ref[:, :KV_LORA] = ckv
    kv_ref[:, KV_LORA:] = kr
    kvb_ref[:, :KV_LORA] = ckv.astype(BF16)
    lane = lax.broadcasted_iota(jnp.int32, (kr.shape[0], KV_PAD - KV_LORA), 1)
    tail = jnp.where(lane == QK_ROPE, 1.0, jnp.concatenate([kr, jnp.zeros_like(kr)], axis=-1))
    kvb_ref[:, KV_LORA:] = tail.astype(BF16)


def mla_proj(s, li, w_cat, g_q, g_kv, cs_tab, norm_pre):
    D, N = w_cat.shape
    out = (jax.ShapeDtypeStruct((s.G, s.R, Q_LORA), BF16),
           jax.ShapeDtypeStruct((s.G, s.R, KV_ROW), F32),
           jax.ShapeDtypeStruct((s.G, s.R, KV_PAD), BF16))
    return pl.pallas_call(
        _mla_proj_kernel,
        grid=(s.G, s.R // s.bm),
        in_specs=[s.rows(D), s.mod_spec(li, 3), s.mod_spec(li, 4), _gain_spec(li * 3 + 1, D),
                  pl.BlockSpec((D, N), lambda g, i: (0, 0)),
                  pl.BlockSpec((1, Q_LORA), lambda g, i: (0, 0)),
                  pl.BlockSpec((1, KV_LORA), lambda g, i: (0, 0)),
                  pl.BlockSpec((s.bm, LANE), lambda g, i: (i, 0))],
        out_specs=(s.rows(Q_LORA), s.rows(KV_ROW), s.rows(KV_PAD)),
        out_shape=out,
        compiler_params=_cparams("parallel", "parallel"),
        name="mla_proj",
    )(s.x, s.mod, s.mod, norm_pre, w_cat, g_q, g_kv, cs_tab)


def _q_expand_kernel(ql_ref, w_ref, wr_ref, cm_ref, sm_ref, q_ref, *, hb):
    ql = ql_ref[...]
    a = _dot(ql, w_ref[...])
    b = _dot(ql, wr_ref[...])
    cm, sm = cm_ref[...], sm_ref[...]
    for h in range(hb):
        sl = slice(h * QK_PAD, (h + 1) * QK_PAD)
        q_ref[:, sl] = (a[:, sl] * cm + b[:, sl] * sm).astype(BF16)


def q_expand(s, ql, w_q, w_qrot, cmap, smap):
    hb = 4
    nw = hb * QK_PAD
    wspec = pl.BlockSpec((Q_LORA, nw), lambda g, i, j: (0, j))
    tspec = pl.BlockSpec((s.bm, QK_PAD), lambda g, i, j: (i, 0))
    return pl.pallas_call(
        functools.partial(_q_expand_kernel, hb=hb),
        grid=(s.G, s.R // s.bm, MLA_HEADS // hb),
        in_specs=[s.rows(Q_LORA), wspec, wspec, tspec, tspec],
        out_specs=pl.BlockSpec((None, s.bm, nw), lambda g, i, j: (g, i, j)),
        out_shape=jax.ShapeDtypeStruct((s.G, s.R, MLA_HEADS * QK_PAD), BF16),
        compiler_params=_cparams("parallel", "parallel", "parallel"),
        name="q_expand",
    )(ql, w_q, w_qrot, cmap, smap)


def _kv_expand_kernel(kvb_ref, wk_ref, wv_ref, k_ref, v_ref):
    kvb = kvb_ref[...]
    k_ref[...] = _dot(kvb, wk_ref[...]).astype(BF16)
    v_ref[...] = _dot(kvb, wv_ref[...]).astype(BF16)


def kv_expand(s, kvb, w_k, w_v):
    hb = 4
    return pl.pallas_call(
        _kv_expand_kernel,
        grid=(s.G, s.R // s.bm, MLA_HEADS // hb),
        in_specs=[s.rows(KV_PAD),
                  pl.BlockSpec((KV_PAD, hb * QK_PAD), lambda g, i, j: (0, j)),
                  pl.BlockSpec((KV_PAD, hb * QK_PAD), lambda g, i, j: (0, j))],
        out_specs=(pl.BlockSpec((None, s.bm, hb * QK_PAD), lambda g, i, j: (g, i, j)),
                   pl.BlockSpec((None, s.bm, hb * QK_PAD), lambda g, i, j: (g, i, j))),
        out_shape=(jax.ShapeDtypeStruct((s.G, s.R, MLA_HEADS * QK_PAD), BF16),
                   jax.ShapeDtypeStruct((s.G, s.R, MLA_HEADS * QK_PAD), BF16)),
        compiler_params=_cparams("parallel", "parallel", "parallel"),
        name="kv_expand",
    )(kvb, w_k, w_v)


def _flash_kernel(q_ref, k_ref, v_ref, o_ref, m_ref, acc_ref, *, blk, hb):
    i = pl.program_id(2)
    m_ref[...] = jnp.full_like(m_ref, NEG_INF)
    acc_ref[...] = jnp.zeros_like(acc_ref)

    def block(j, diagonal):
        r0 = pl.multiple_of(j * blk, blk)
        for h in range(hb):
            hs = slice(h * QK_PAD, (h + 1) * QK_PAD)
            s = _dot_nt(q_ref[:, hs], k_ref[pl.ds(r0, blk), hs])
            if diagonal:
                row = lax.broadcasted_iota(jnp.int32, (blk, blk), 0)
                col = lax.broadcasted_iota(jnp.int32, (blk, blk), 1)
                s = jnp.where(col <= row, s, NEG_INF)
            m_prev = m_ref[h]
            m_new = jnp.maximum(m_prev, jnp.max(s, axis=-1, keepdims=True))
            alpha = jnp.exp(m_prev - m_new)
            p = jnp.exp(s - jnp.concatenate([m_new] * (blk // LANE), axis=1))
            pv = _dot(p.astype(BF16), v_ref[pl.ds(r0, blk), hs])
            acc_ref[h] = jnp.concatenate([alpha] * (QK_PAD // LANE), axis=1) * acc_ref[h] + pv
            m_ref[h] = m_new

    def below_diagonal(jj, carry):
        block(2 * jj, False)
        block(2 * jj + 1, False)
        return carry

    lax.fori_loop(0, i // 2, below_diagonal, 0)

    @pl.when(i % 2 == 1)
    def _():
        block(i - 1, False)

    block(i, True)
    for h in range(hb):
        acc = acc_ref[h]
        o_ref[:, h * V_HEAD:(h + 1) * V_HEAD] = (acc[:, :V_HEAD] / acc[:, V_HEAD:V_HEAD + 1]).astype(o_ref.dtype)


def flash_prompt(q, k, v):
    B, L, _ = q.shape
    blk = min(512, L)
    hb = 4
    seq = pl.BlockSpec((None, L, hb * QK_PAD), lambda b, h, i: (b, 0, h))
    return pl.pallas_call(
        functools.partial(_flash_kernel, blk=blk, hb=hb),
        grid=(B, MLA_HEADS // hb, L // blk),
        in_specs=[pl.BlockSpec((None, blk, hb * QK_PAD), lambda b, h, i: (b, i, h)), seq, seq],
        out_specs=pl.BlockSpec((None, blk, hb * V_HEAD), lambda b, h, i: (b, i, h)),
        out_shape=jax.ShapeDtypeStruct((B, L, MLA_HEADS * V_HEAD), BF16),
        scratch_shapes=[pltpu.VMEM((hb, blk, LANE), F32), pltpu.VMEM((hb, blk, QK_PAD), F32)],
        compiler_params=_cparams("parallel", "parallel", "parallel"),
        name="flash_prompt",
    )(q, k, v)


def _q_absorb_kernel(q_ref, w_ref, qc_ref):
    q = q_ref[...]
    qc_ref[:, :KV_LORA] = _dot(q[:, :QK_NOPE], w_ref[...]).astype(BF16)
    qc_ref[:, KV_LORA:] = q[:, QK_NOPE:]


def q_absorb(q, w_ukT):
    R = q.shape[0]
    return pl.pallas_call(
        _q_absorb_kernel,
        grid=(MLA_HEADS,),
        in_specs=[pl.BlockSpec((R, QK_PAD), lambda h: (0, h)),
                  pl.BlockSpec((None, QK_NOPE, KV_LORA), lambda h: (h, 0, 0))],
        out_specs=pl.BlockSpec((None, R, KV_PAD), lambda h: (h, 0, 0)),
        out_shape=jax.ShapeDtypeStruct((MLA_HEADS, R, KV_PAD), BF16),
        compiler_params=_cparams("parallel"),
        name="q_absorb",
    )(q, w_ukT)


PAGE_SLOTS = 3


def _paged_kernel(pt_ref, qc_ref, newt_ref, cache_ref, o_ref, buf_ref, sem_ref, kb_ref, kn_ref, m_ref, l_ref, acc_ref,
                  *, layer, pp, group, steps_per_seq, heads):
    t = pl.program_id(0)
    n_steps = pl.num_programs(0)
    j = t % steps_per_seq
    rows = qc_ref.shape[0]
    gw = group * PAGE_SIZE

    def page_copy(step, k):
        slot = step % PAGE_SLOTS
        page = pt_ref[step * pp + k]
        return pltpu.make_async_copy(cache_ref.at[layer, page], buf_ref.at[slot, k], sem_ref.at[slot])

    def fetch(step):
        for k in range(pp):
            page_copy(step, k).start()

    @pl.when(t == 0)
    def _():
        for ahead in range(PAGE_SLOTS - 1):
            @pl.when(ahead < n_steps)
            def _():
                fetch(ahead)

    @pl.when(t + (PAGE_SLOTS - 1) < n_steps)
    def _():
        fetch(t + (PAGE_SLOTS - 1))

    @pl.when(j == 0)
    def _():
        m_ref[...] = jnp.full_like(m_ref, NEG_INF)
        l_ref[...] = jnp.zeros_like(l_ref)
        acc_ref[...] = jnp.zeros_like(acc_ref)
        kb_ref[KV_ROW:, :] = jnp.zeros((KV_PAD - KV_ROW, kb_ref.shape[1]), BF16)

    def update(key_refs, visible):
        q = qc_ref[...]
        ss = [_dot(q, kr[...]) for kr in key_refs]
        if visible is not None:
            ss = [jnp.where(visible, s, NEG_INF) for s in ss]
        m_prev = m_ref[...]
        m_new = m_prev
        for s in ss:
            m_new = jnp.maximum(m_new, jnp.max(s, axis=-1, keepdims=True))
        alpha = jnp.exp(m_prev - m_new)
        l_new = alpha * l_ref[...]
        acc = alpha * acc_ref[...]
        for s, kr in zip(ss, key_refs):
            p = jnp.exp(s - m_new)
            l_new = l_new + jnp.sum(p, axis=-1, keepdims=True)
            acc = acc + _dot_nt(p.astype(BF16), kr[:KV_LORA, :])
        l_ref[...] = l_new
        acc_ref[...] = acc
        m_ref[...] = m_new

    slot = t % PAGE_SLOTS
    for k in range(pp):
        page_copy(t, k).wait()
    for k in range(pp):
        kb_ref[:KV_ROW, k * PAGE_SIZE:(k + 1) * PAGE_SIZE] = buf_ref[slot, k].astype(BF16)
    update([kb_ref.at[:, g * gw:(g + 1) * gw] for g in range(pp // group)], None)

    @pl.when(j == steps_per_seq - 1)
    def _():
        T = newt_ref.shape[1]
        kn_ref[...] = jnp.zeros_like(kn_ref)
        kn_ref[:KV_ROW, :T] = newt_ref[...].astype(BF16)
        row = lax.broadcasted_iota(jnp.int32, (rows, PAGE_SIZE), 0)
        col = lax.broadcasted_iota(jnp.int32, (rows, PAGE_SIZE), 1)
        update([kn_ref], col * heads <= row)
        o_ref[...] = acc_ref[...] / l_ref[...]


def paged_attention(qc, kv_new, cache_t, layer, page_table):
    Bd, rows, _ = qc.shape
    T = kv_new.shape[1]
    n_pages = page_table.shape[1]
    pp = 32 if n_pages % 32 == 0 else n_pages
    group = 8 if pp % 8 == 0 else pp
    sps = n_pages // pp
    grid_spec = pltpu.PrefetchScalarGridSpec(
        num_scalar_prefetch=1,
        grid=(Bd * sps,),
        in_specs=[pl.BlockSpec((None, rows, KV_PAD), lambda t, pt: (t // sps, 0, 0)),
                  pl.BlockSpec((None, KV_ROW, T), lambda t, pt: (t // sps, 0, 0)),
                  pl.BlockSpec(memory_space=pl.ANY)],
        out_specs=pl.BlockSpec((None, rows, KV_LORA), lambda t, pt: (t // sps, 0, 0)),
        scratch_shapes=[pltpu.VMEM((PAGE_SLOTS, pp, KV_ROW, PAGE_SIZE), F32), pltpu.SemaphoreType.DMA((PAGE_SLOTS,)),
                        pltpu.VMEM((KV_PAD, pp * PAGE_SIZE), BF16), pltpu.VMEM((KV_PAD, PAGE_SIZE), BF16),
                        pltpu.VMEM((rows, 1), F32), pltpu.VMEM((rows, 1), F32), pltpu.VMEM((rows, KV_LORA), F32)])
    return pl.pallas_call(
        functools.partial(_paged_kernel, layer=layer, pp=pp, group=group, steps_per_seq=sps, heads=rows // T),
        grid_spec=grid_spec,
        out_shape=jax.ShapeDtypeStruct((Bd, rows, KV_LORA), F32),
        compiler_params=_cparams("arbitrary"),
        name="paged_attention",
    )(page_table.reshape(-1), qc, kv_new.transpose(0, 2, 1), cache_t)


def _v_up_kernel(o_ref, w_ref, y_ref):
    y_ref[...] = _dot(o_ref[...].astype(BF16), w_ref[...]).astype(BF16)


def v_up(o_lat, w_uv):
    R = o_lat.shape[0]
    return pl.pallas_call(
        _v_up_kernel,
        grid=(MLA_HEADS,),
        in_specs=[pl.BlockSpec((R, KV_LORA), lambda h: (0, h)), pl.BlockSpec((KV_LORA, V_HEAD), lambda h: (0, h))],
        out_specs=pl.BlockSpec((R, V_HEAD), lambda h: (0, h)),
        out_shape=jax.ShapeDtypeStruct((R, MLA_HEADS * V_HEAD), BF16),
        compiler_params=_cparams("parallel"),
        name="v_up",
    )(o_lat, w_uv)


def _hg_proj_kernel(x_ref, sh_ref, sc_ref, gpre_ref, wq_ref, wf_ref, wi_ref, wg_ref, lb_ref,
                    q_ref, k_ref, v_ref, lf_ref, g_ref, h_ref):
    @pl.when(pl.program_id(2) == 0)
    def _():
        _prenorm_into(h_ref, x_ref, sh_ref, sc_ref, gpre_ref)

    h = h_ref[...]
    q_ref[...] = _silu(_dot(h, wq_ref[...])).astype(q_ref.dtype)
    fz = _dot(h, wf_ref[...])
    lb = lb_ref[...]
    a = jnp.log(lb)
    b = jnp.log1p(-lb) + (jnp.minimum(fz, 0.0) - jnp.log1p(jnp.exp(-jnp.abs(fz))))
    lf_ref[...] = jnp.maximum(a, b) + jnp.log1p(jnp.exp(-jnp.abs(a - b)))
    k_ref[...] = ((1.0 - lb) * _sigmoid(-fz)).astype(k_ref.dtype)
    v_ref[...] = _dot(h, wi_ref[...]).astype(v_ref.dtype)
    g_ref[...] = _silu(_dot(h, wg_ref[...])).astype(g_ref.dtype)


PROJ_BM = 1024
PROJ_BN = 256


def _blocked(w):
    D, N = w.shape
    return w.reshape(D, N // PROJ_BN, PROJ_BN).transpose(1, 0, 2).astype(BF16)


def hg_proj(s, li, w_q, w_f, w_i, w_g, lb, norm_pre):
    s = Stream(s.x, s.mod, s.per_row, bm=min(PROJ_BM, s.R))
    nb, D, bn = w_q.shape
    wspec = pl.BlockSpec((None, D, bn), lambda g, i, j: (j, 0, 0))
    ospec = pl.BlockSpec((None, s.bm, bn), lambda g, i, j: (g, i, j))
    shape = lambda dt: jax.ShapeDtypeStruct((s.G, s.R, nb * bn), dt)
    return pl.pallas_call(
        _hg_proj_kernel,
        grid=(s.G, s.R // s.bm, nb),
        in_specs=[s.rows(D), s.mod_spec(li, 3), s.mod_spec(li, 4), _gain_spec(li * 3 + 1, D),
                  wspec, wspec, wspec, wspec, pl.BlockSpec((1, bn), lambda g, i, j: (0, j))],
        out_specs=(ospec,) * 5,
        out_shape=(shape(BF16), shape(BF16), shape(BF16), shape(F32), shape(BF16)),
        scratch_shapes=[pltpu.VMEM((s.bm, D), BF16)],
        compiler_params=_cparams("parallel", "parallel", "arbitrary"),
        name="hg_proj",
    )(s.x, s.mod, s.mod, norm_pre, w_q, w_f, w_i, w_g, lb)


def _cumsum_rows(x, span):
    row = lax.broadcasted_iota(jnp.int32, x.shape, 0)
    pos = jnp.bitwise_and(row, span - 1)
    sh = 1
    while sh < span:
        x = x + jnp.where(pos >= sh, pltpu.roll(x, sh, 0), 0.0)
        sh *= 2
    return x


def _chunk_id(r):
    return lax.shift_right_logical(r, HG_CHUNK.bit_length() - 1)


def _hg_scan_kernel(q_ref, k_ref, v_ref, lf_ref, g_ref, s0_ref, gn_ref, o_ref, so_ref, st_ref, *, hb, chained):
    n = HG_BLOCK // HG_CHUNK
    first = pl.program_id(2) == 0
    last = pl.program_id(2) == pl.num_programs(2) - 1

    if chained:
        @pl.when(first)
        def _():
            for h in range(hb):
                st_ref[h] = s0_ref[h].T

    row = lax.broadcasted_iota(jnp.int32, (HG_BLOCK, HG_BLOCK), 0)
    col = lax.broadcasted_iota(jnp.int32, (HG_BLOCK, HG_BLOCK), 1)
    causal = (_chunk_id(row) == _chunk_id(col)) & (col <= row)
    chunk_of_row = _chunk_id(lax.broadcasted_iota(jnp.int32, (HG_BLOCK, HG_DK), 0))

    for h in range(hb):
        sl = slice(h * HG_DK, (h + 1) * HG_DK)
        b = _cumsum_rows(lf_ref[:, sl], HG_CHUNK)
        b_last = jnp.concatenate(
            [jnp.broadcast_to(b[c * HG_CHUNK + HG_CHUNK - 1:(c + 1) * HG_CHUNK, :], (HG_CHUNK, HG_DK)) for c in range(n)],
            axis=0)
        q, k, v = q_ref[:, sl].astype(F32), k_ref[:, sl].astype(F32), v_ref[:, sl].astype(F32)
        qg = q * jnp.exp(b)
        kd = (k * jnp.exp(-b)).astype(BF16)
        kl = k * jnp.exp(b_last - b)
        qgb = qg.astype(BF16)
        vb = v.astype(BF16)
        a = jnp.where(causal, _dot_nt(qgb, kd), 0.0)
        o = _dot(a.astype(BF16), vb)
        klx = jnp.concatenate([jnp.where(chunk_of_row == c, kl, 0.0).astype(BF16) for c in range(n)], axis=1)
        qx = jnp.concatenate([jnp.where(chunk_of_row == c, qg, 0.0).astype(BF16) for c in range(n)], axis=1)
        if chained:
            ut = _dot(v.T.astype(BF16), klx)
            sts = []
            st = st_ref[h]
            for c in range(n):
                sts.append(st.astype(BF16))
                dec = jnp.exp(b_last[c * HG_CHUNK:c * HG_CHUNK + 1, :])
                st = st * dec + ut[:, c * HG_DK:(c + 1) * HG_DK]
            st_ref[h] = st

            @pl.when(last)
            def _():
                so_ref[h] = st.T
            o = o + _dot_nt(qx, jnp.concatenate(sts, axis=1))
        else:
            s0 = [s0_ref[c, h] for c in range(n)]
            dec_rows = jnp.concatenate([b_last[c * HG_CHUNK:c * HG_CHUNK + 1, :] for c in range(n)], axis=0)
            dec_cols = jnp.exp(dec_rows).T
            u_all = lax.dot_general(klx, vb, (((0,), (0,)), ((), ())), preferred_element_type=F32)
            for c in range(n):
                so_ref[c, h] = s0[c] * dec_cols[:, c:c + 1] + u_all[c * HG_DK:(c + 1) * HG_DK, :]
            o = o + _dot(qx, jnp.concatenate([s.astype(BF16) for s in s0], axis=0))
        o_ref[:, sl] = (_rms(o, gn_ref[...]) * g_ref[:, sl]).astype(BF16)


def hg_scan(q, k, v, lf, gate, s0, g_norm, chained):
    G, R, N = q.shape
    hb = 8 if chained else 4
    n = HG_BLOCK // HG_CHUNK
    rows = pl.BlockSpec((None, HG_BLOCK, hb * HG_DK), lambda g, h, l: (g, l, h))
    if chained:
        sspec = pl.BlockSpec((None, hb, HG_DK, HG_DV), lambda g, h, l: (g, h, 0, 0))
    else:
        sspec = pl.BlockSpec((None, n, hb, HG_DK, HG_DV), lambda g, h, l: (g, 0, h, 0, 0))
    return pl.pallas_call(
        functools.partial(_hg_scan_kernel, hb=hb, chained=chained),
        grid=(G, HG_HEADS // hb, R // HG_BLOCK),
        in_specs=[rows, rows, rows, rows, rows, sspec, pl.BlockSpec((1, HG_DV), lambda g, h, l: (0, 0))],
        out_specs=(rows, sspec),
        out_shape=(jax.ShapeDtypeStruct((G, R, N), BF16), jax.ShapeDtypeStruct(s0.shape, F32)),
        scratch_shapes=[pltpu.VMEM((hb, HG_DV, HG_DK), F32)],
        compiler_params=_cparams("parallel", "parallel", "arbitrary"),
        name="hg_scan",
    )(q, k, v, lf, gate, s0, g_norm)


def _lru_proj_kernel(x_ref, sh_ref, sc_ref, gpre_ref, wx_ref, wy_ref, u_ref, g_ref, h_ref):
    @pl.when(pl.program_id(2) == 0)
    def _():
        _prenorm_into(h_ref, x_ref, sh_ref, sc_ref, gpre_ref)

    h = h_ref[...]
    u_ref[...] = _dot(h, wx_ref[...])
    g_ref[...] = _gelu_tanh(_dot(h, wy_ref[...])).astype(g_ref.dtype)


def lru_proj(s, li, w_x, w_y, norm_pre):
    s = Stream(s.x, s.mod, s.per_row, bm=min(PROJ_BM, s.R))
    nb, D, bn = w_x.shape
    wspec = pl.BlockSpec((None, D, bn), lambda g, i, j: (j, 0, 0))
    ospec = pl.BlockSpec((None, s.bm, bn), lambda g, i, j: (g, i, j))
    return pl.pallas_call(
        _lru_proj_kernel,
        grid=(s.G, s.R // s.bm, nb),
        in_specs=[s.rows(D), s.mod_spec(li, 3), s.mod_spec(li, 4), _gain_spec(li * 3 + 1, D), wspec, wspec],
        out_specs=(ospec, ospec),
        out_shape=(jax.ShapeDtypeStruct((s.G, s.R, nb * bn), F32), jax.ShapeDtypeStruct((s.G, s.R, nb * bn), BF16)),
        scratch_shapes=[pltpu.VMEM((s.bm, D), BF16)],
        compiler_params=_cparams("parallel", "parallel", "arbitrary"),
        name="lru_proj",
    )(s.x, s.mod, s.mod, norm_pre, w_x, w_y)


def _lru_gates(xc, n, wra_ref, bra_ref, wix_ref, bix_ref, lam_ref):
    sl = slice(n * LRU_BW, (n + 1) * LRU_BW)
    xb = xc.astype(BF16)
    r = _sigmoid(_dot(xb, wra_ref[n]) + bra_ref[:, sl])
    ig = _sigmoid(_dot(xb, wix_ref[n]) + bix_ref[:, sl])
    log_a = -LRU_C * r * _softplus(-lam_ref[:, sl])
    a = jnp.exp(log_a)
    return a, jnp.sqrt(-jnp.tanh(log_a) * (a * a + 1.0)) * (ig * xc)


def _lru_scan_kernel(u_ref, gate_ref, cb_ref, h0_ref, cw_ref, cbias_ref, wra_ref, bra_ref, wix_ref, bix_ref, lam_ref,
                     yg_ref, hl_ref, cbo_ref, ext_ref, a_ref, b_ref, hc_ref, *, bl):
    l = pl.program_id(1)
    keep = CONV_W - 1
    base = 8

    @pl.when(l == 0)
    def _():
        ext_ref[base - keep:base, :] = cb_ref[...]
        hc_ref[...] = h0_ref[...]

    ext_ref[base:base + bl, :] = u_ref[...]
    for n in range(LRU_BLOCKS):
        sl = slice(n * LRU_BW, (n + 1) * LRU_BW)
        xc = cbias_ref[:, sl]
        for j in range(CONV_W):
            xc = xc + ext_ref[base - keep + j:base - keep + j + bl, sl] * cw_ref[j:j + 1, sl]
        a, b = _lru_gates(xc, n, wra_ref, bra_ref, wix_ref, bix_ref, lam_ref)
        a_ref[:, sl] = a
        b_ref[:, sl] = b

    pos = lax.broadcasted_iota(jnp.int32, (8, a_ref.shape[1]), 0)

    def tile(t, h_prev):
        r0 = pl.multiple_of(t * 8, 8)
        a_t = a_ref[pl.ds(r0, 8), :]
        b_t = b_ref[pl.ds(r0, 8), :]
        for sh in (1, 2, 4):
            m = pos >= sh
            b_t = jnp.where(m, a_t * pltpu.roll(b_t, sh, 0) + b_t, b_t)
            a_t = jnp.where(m, a_t * pltpu.roll(a_t, sh, 0), a_t)
        h_t = b_t + a_t * h_prev
        b_ref[pl.ds(r0, 8), :] = h_t
        return h_t[7:8, :]

    h_last = lax.fori_loop(0, bl // 8, tile, hc_ref[...])
    hc_ref[...] = h_last
    yg_ref[...] = (b_ref[...] * gate_ref[...]).astype(BF16)
    tail = ext_ref[base + bl - keep:base + bl, :]
    ext_ref[base - keep:base, :] = tail

    @pl.when(l == pl.num_programs(1) - 1)
    def _():
        hl_ref[...] = h_last
        cbo_ref[...] = tail


def lru_scan(u, gate, conv_buf, h0, conv_w, conv_b, w_ra, b_ra, w_ix, b_ix, lam):
    B, L, W = u.shape
    bl = min(256, L)
    rows = pl.BlockSpec((None, bl, W), lambda b, l: (b, l, 0))
    vec = pl.BlockSpec((1, W), lambda b, l: (0, 0))
    wblk = pl.BlockSpec((LRU_BLOCKS, LRU_BW, LRU_BW), lambda b, l: (0, 0, 0))
    hspec = pl.BlockSpec((None, 1, W), lambda b, l: (b, 0, 0))
    cspec = pl.BlockSpec((None, CONV_W - 1, W), lambda b, l: (b, 0, 0))
    return pl.pallas_call(
        functools.partial(_lru_scan_kernel, bl=bl),
        grid=(B, L // bl),
        in_specs=[rows, rows, cspec, hspec, pl.BlockSpec((CONV_W, W), lambda b, l: (0, 0)), vec,
                  wblk, vec, wblk, vec, vec],
        out_specs=(rows, hspec, cspec),
        out_shape=(jax.ShapeDtypeStruct((B, L, W), BF16), jax.ShapeDtypeStruct((B, 1, W), F32),
                   jax.ShapeDtypeStruct((B, CONV_W - 1, W), F32)),
        scratch_shapes=[pltpu.VMEM((bl + 8, W), F32), pltpu.VMEM((bl, W), F32), pltpu.VMEM((bl, W), F32),
                        pltpu.VMEM((1, W), F32)],
        compiler_params=_cparams("parallel", "arbitrary"),
        name="lru_scan",
    )(u, gate, conv_buf, h0.reshape(B, 1, W), conv_w, conv_b, w_ra, b_ra, w_ix, b_ix, lam)


def _lru_step_kernel(u_ref, gate_ref, cb_ref, h0_ref, cw_ref, cbias_ref, wra_ref, bra_ref, wix_ref, bix_ref, lam_ref,
                     yg_ref, hl_ref, cbo_ref):
    T = u_ref.shape[0]
    keep = CONV_W - 1
    for n in range(LRU_BLOCKS):
        sl = slice(n * LRU_BW, (n + 1) * LRU_BW)
        ext = [cb_ref[j, :, sl] for j in range(keep)] + [u_ref[t, :, sl] for t in range(T)]
        h = h0_ref[:, sl]
        for t in range(T):
            xc = cbias_ref[:, sl]
            for j in range(CONV_W):
                xc = xc + ext[t + j] * cw_ref[j:j + 1, sl]
            a, b = _lru_gates(xc, n, wra_ref, bra_ref, wix_ref, bix_ref, lam_ref)
            h = a * h + b
            yg_ref[t, :, sl] = (h * gate_ref[t, :, sl]).astype(BF16)
        hl_ref[:, sl] = h
        for j in range(keep):
            cbo_ref[j, :, sl] = ext[T + j]


def lru_step(u, gate, conv_buf, h0, conv_w, conv_b, w_ra, b_ra, w_ix, b_ix, lam):
    T, Bd, W = u.shape
    full = lambda *shape: pl.BlockSpec(shape, lambda i: (0,) * len(shape))
    return pl.pallas_call(
        _lru_step_kernel,
        grid=(1,),
        in_specs=[full(T, Bd, W), full(T, Bd, W), full(CONV_W - 1, Bd, W), full(Bd, W), full(CONV_W, W), full(1, W),
                  full(LRU_BLOCKS, LRU_BW, LRU_BW), full(1, W), full(LRU_BLOCKS, LRU_BW, LRU_BW), full(1, W), full(1, W)],
        out_specs=(full(T, Bd, W), full(Bd, W), full(CONV_W - 1, Bd, W)),
        out_shape=(jax.ShapeDtypeStruct((T, Bd, W), BF16), jax.ShapeDtypeStruct((Bd, W), F32),
                   jax.ShapeDtypeStruct((CONV_W - 1, Bd, W), F32)),
        compiler_params=_cparams("arbitrary"),
        name="lru_step",
    )(u, gate, conv_buf, h0, conv_w, conv_b, w_ra, b_ra, w_ix, b_ix, lam)


def _rope_tables(pos):
    half = QK_ROPE // 2
    inv = ROPE_THETA ** (-jnp.arange(half, dtype=F32) / half)
    ang = pos.astype(F32)[:, None] * inv[None, :]
    cos, sin = jnp.cos(ang), jnp.sin(ang)
    cos2, sin2 = jnp.concatenate([cos, cos], -1), jnp.concatenate([sin, sin], -1)
    ones, zeros = jnp.ones((pos.shape[0], QK_NOPE), F32), jnp.zeros((pos.shape[0], QK_NOPE), F32)
    pad = jnp.zeros((pos.shape[0], QK_PAD - QK_NOPE - QK_ROPE), F32)
    cs_tab = jnp.concatenate([cos2, sin2], -1)
    cmap = jnp.concatenate([ones, cos2, pad], -1) * MLA_SCALE
    smap = jnp.concatenate([zeros, sin2, pad], -1) * MLA_SCALE
    return cs_tab, cmap, smap


def _rot_cols(w):
    half = w.shape[-1] // 2
    return jnp.concatenate([-w[..., half:], w[..., :half]], -1)


def _mla_weights(w_dq, w_uq, w_dkv, w_uk, w_uv):
    kr = w_dkv[:, KV_LORA:]
    w_cat = jnp.concatenate([w_dq, w_dkv, _rot_cols(kr)], axis=1).astype(BF16)
    uq = w_uq.reshape(Q_LORA, MLA_HEADS, QK_NOPE + QK_ROPE)
    nope, rope = uq[..., :QK_NOPE], uq[..., QK_NOPE:]
    pad = jnp.zeros((Q_LORA, MLA_HEADS, QK_PAD - QK_NOPE - QK_ROPE), F32)
    w_q = jnp.concatenate([nope, rope, pad], -1).reshape(Q_LORA, MLA_HEADS * QK_PAD).astype(BF16)
    w_qrot = jnp.concatenate([jnp.zeros_like(nope), _rot_cols(rope), pad], -1)
    w_qrot = w_qrot.reshape(Q_LORA, MLA_HEADS * QK_PAD).astype(BF16)
    w_k = jnp.zeros((KV_PAD, MLA_HEADS, QK_PAD), F32)
    w_k = w_k.at[:KV_LORA, :, :QK_NOPE].set(w_uk)
    eye = jnp.broadcast_to(jnp.eye(QK_ROPE, dtype=F32)[:, None, :], (QK_ROPE, MLA_HEADS, QK_ROPE))
    w_k = w_k.at[KV_LORA:KV_ROW, :, QK_NOPE:QK_NOPE + QK_ROPE].set(eye)
    w_k = w_k.reshape(KV_PAD, MLA_HEADS * QK_PAD).astype(BF16)
    w_v = jnp.zeros((KV_PAD, MLA_HEADS, QK_PAD), F32)
    w_v = w_v.at[:KV_LORA, :, :V_HEAD].set(w_uv).at[KV_ROW, :, V_HEAD].set(1.0)
    w_v = w_v.reshape(KV_PAD, MLA_HEADS * QK_PAD).astype(BF16)
    w_ukT = w_uk.transpose(1, 2, 0).astype(BF16)
    w_uv2 = w_uv.reshape(KV_LORA, MLA_HEADS * V_HEAD).astype(BF16)
    return w_cat, w_q, w_qrot, w_k, w_v, w_ukT, w_uv2


def _hg_lower_bounds(lb_param):
    cs = jnp.cumsum(jax.nn.softmax(lb_param.astype(F32), axis=0), axis=0)
    return cs - cs[0]


def kernel(x_prompt, x_sample, c_prompt, c_sample, cache_mla_kv, page_table, state_hgrn, state_lru_h, state_lru_conv, ada_w, ada_b, norm_pre, norm_post, ffn1_wg, ffn1_wu, ffn1_wd, ffn2_wg, ffn2_wu, ffn2_wd, mla_w_dq, mla_g_q, mla_w_uq, mla_w_dkv, mla_g_kv, mla_w_uk, mla_w_uv, mla_w_o, hg_lb, hg_w_q, hg_w_f, hg_w_i, hg_w_g, hg_g_norm, hg_w_o, lru_w_x, lru_w_y, lru_conv_w, lru_conv_b, lru_w_ra, lru_b_ra, lru_w_ix, lru_b_ix, lru_lam, lru_w_o):
    B, L, D = x_prompt.shape
    Bd, T, _ = x_sample.shape
    n_pages = page_table.shape[1]
    past_len = n_pages * PAGE_SIZE

    n_s = Bd * T
    c_all = jnp.concatenate([jnp.repeat(c_sample, T, axis=0), c_prompt, jnp.zeros((-(n_s + B) % 8, D), F32)], axis=0)
    mod = ada_mod(c_all, ada_w, ada_b)
    mod_p = mod[:, n_s:n_s + B].reshape(DEPTH * B, 1, N_MOD * D)
    sp = Stream(x_prompt, mod_p, per_row=False)
    ss = Stream(x_sample.reshape(1, n_s, D), mod, per_row=True)

    npre = norm_pre.reshape(DEPTH * 3, 1, D)
    npost = norm_post.reshape(DEPTH * 3, 1, D)
    bf = lambda w: w.astype(BF16)
    f1 = (ffn1_wg, ffn1_wu, ffn1_wd)
    f2 = (ffn2_wg, ffn2_wu, ffn2_wd)
    cache_t = jnp.swapaxes(cache_mla_kv, 2, 3)

    tab_p = _rope_tables(jnp.arange(L))
    tab_s = _rope_tables(jnp.tile(past_len + jnp.arange(T), Bd))
    lbs = _hg_lower_bounds(hg_lb)

    kv_p, kv_s, hg_p, hg_s, lh_p, lh_s, lc_p, lc_s = [], [], [], [], [], [], [], []
    for li in range(DEPTH):
        xs, w_bf = ffn_emit(ss, li, 0, *f1, npre, npost)
        ss = ss.with_x(xs)
        sp = sp.with_x(ffn(sp, li, 0, *w_bf, npre, npost))
        kind, j = li % N_MIXERS, li // N_MIXERS
        if kind == 0:
            w_cat, w_q, w_qrot, w_k, w_v, w_ukT, w_uv2 = _mla_weights(
                mla_w_dq[j], mla_w_uq[j], mla_w_dkv[j], mla_w_uk[j], mla_w_uv[j])
            g_q, g_kv = mla_g_q[j].reshape(1, Q_LORA), mla_g_kv[j].reshape(1, KV_LORA)
            w_o = bf(mla_w_o[j])
            ql, kvr, kvb = mla_proj(sp, li, w_cat, g_q, g_kv, tab_p[0], npre)
            q = q_expand(sp, ql, w_q, w_qrot, tab_p[1], tab_p[2])
            kk, vv = kv_expand(sp, kvb, w_k, w_v)
            o = flash_prompt(q, kk, vv)
            sp = sp.with_x(mm_post(sp, li, o, w_o, npost))
            kv_p.append(kvr)
            ql, kvr, _ = mla_proj(ss, li, w_cat, g_q, g_kv, tab_s[0], npre)
            q = q_expand(ss, ql, w_q, w_qrot, tab_s[1], tab_s[2])
            qc = q_absorb(q[0], w_ukT)
            qc = qc.reshape(MLA_HEADS, Bd, T, KV_PAD).transpose(1, 2, 0, 3).reshape(Bd, T * MLA_HEADS, KV_PAD)
            kv_new = kvr.reshape(Bd, T, KV_ROW)
            o_lat = paged_attention(qc, kv_new, cache_t, j, page_table)
            o = v_up(o_lat.reshape(Bd * T, MLA_HEADS * KV_LORA), w_uv2)
            ss = ss.with_x(mm_post(ss, li, o.reshape(1, Bd * T, -1), w_o, npost))
            kv_s.append(kv_new)
        elif kind == 1:
            lb = lbs[li].reshape(1, -1)
            ws = (_blocked(hg_w_q[j]), _blocked(hg_w_f[j]), _blocked(hg_w_i[j]), _blocked(hg_w_g[j]))
            g_n = hg_g_norm[j].reshape(1, HG_DV)
            w_o = bf(hg_w_o[j])
            q, k, v, lf, gate = hg_proj(sp, li, *ws, lb, npre)
            s0 = jnp.zeros((B, HG_HEADS, HG_DK, HG_DV), F32)
            o, s_new = hg_scan(q, k, v, lf, gate, s0, g_n, chained=True)
            sp = sp.with_x(mm_post(sp, li, o, w_o, npost))
            hg_p.append(s_new)
            n_seq = HG_BLOCK // HG_CHUNK
            def to_chunks(a):
                a = jnp.pad(a.reshape(Bd, T, -1), ((0, 0), (HG_CHUNK - T, 0), (0, 0)))
                return a.reshape(Bd // n_seq, HG_BLOCK, -1)
            qs, ks, vs, lfs, gs = [to_chunks(a) for a in hg_proj(ss, li, *ws, lb, npre)]
            s0 = state_hgrn[j].reshape(Bd // n_seq, n_seq, HG_HEADS, HG_DK, HG_DV)
            o, s_new = hg_scan(qs, ks, vs, lfs, gs, s0, g_n, chained=False)
            o = o.reshape(Bd, HG_CHUNK, -1)[:, HG_CHUNK - T:].reshape(1, Bd * T, -1)
            ss = ss.with_x(mm_post(ss, li, o, w_o, npost))
            hg_s.append(s_new.reshape(Bd, HG_HEADS, HG_DK, HG_DV))
        else:
            w_x, w_y, w_o = _blocked(lru_w_x[j]), _blocked(lru_w_y[j]), bf(lru_w_o[j])
            W = w_o.shape[0]
            prm = (lru_conv_w[j], lru_conv_b[j].reshape(1, W), bf(lru_w_ra[j]), lru_b_ra[j].reshape(1, W),
                   bf(lru_w_ix[j]), lru_b_ix[j].reshape(1, W), lru_lam[j].reshape(1, W))
            u, gate = lru_proj(sp, li, w_x, w_y, npre)
            yg, h_last, cb = lru_scan(u, gate, jnp.zeros((B, CONV_W - 1, W), F32), jnp.zeros((B, W), F32), *prm)
            sp = sp.with_x(mm_post(sp, li, yg, w_o, npost))
            lh_p.append(h_last.reshape(B, W))
            lc_p.append(cb)
            u, gate = lru_proj(ss, li, w_x, w_y, npre)
            tm = lambda a: a.reshape(Bd, T, W).transpose(1, 0, 2)
            yg, h_last, cb = lru_step(tm(u), tm(gate), state_lru_conv[j].transpose(1, 0, 2), state_lru_h[j], *prm)
            ss = ss.with_x(mm_post(ss, li, yg.transpose(1, 0, 2).reshape(1, Bd * T, W), w_o, npost))
            lh_s.append(h_last)
            lc_s.append(cb.transpose(1, 0, 2))
        xs, w_bf = ffn_emit(ss, li, 2, *f2, npre, npost)
        ss = ss.with_x(xs)
        sp = sp.with_x(ffn(sp, li, 2, *w_bf, npre, npost))

    return (sp.x, ss.x.reshape(Bd, T, D), jnp.stack(kv_p), jnp.stack(kv_s), jnp.stack(hg_p), jnp.stack(hg_s),
            jnp.stack(lh_p), jnp.stack(lh_s), jnp.stack(lc_p), jnp.stack(lc_s))
```

```python
import functools

import jax
import jax.numpy as jnp
from jax import lax
from jax.experimental import pallas as pl
from jax.experimental.pallas import tpu as pltpu

F32 = jnp.float32
BF16 = jnp.bfloat16

DEPTH = 4
N_MIXERS = 3
NORM_EPS = 1e-6
N_MOD = 9
FFN_RES = 0.5

MLA_HEADS = 16
Q_LORA = 512
KV_LORA = 512
QK_NOPE = 128
QK_ROPE = 64
V_HEAD = 128
KV_ROW = KV_LORA + QK_ROPE
KV_PAD = 640
QK_PAD = 256
MLA_SCALE = (QK_NOPE + QK_ROPE) ** -0.5
ROPE_THETA = 10000.0
PAGE_SIZE = 128
NEG_INF = -1e30

HG_HEADS = 16
HG_DK = 128
HG_DV = 128
HG_CHUNK = 16
HG_BLOCK = 128

LRU_BLOCKS = 16
LRU_BW = 128
CONV_W = 4
LRU_C = 8.0

LANE = 128
VMEM_LIMIT = 56 * 1024 * 1024


def _cparams(*sem):
    return pltpu.CompilerParams(dimension_semantics=sem, vmem_limit_bytes=VMEM_LIMIT)


def _dot(a, b):
    return jnp.dot(a, b, preferred_element_type=F32)


def _dot_nt(a, b):
    return lax.dot_general(a, b, (((1,), (1,)), ((), ())), preferred_element_type=F32)


def _rms(x, g):
    ms = jnp.mean(x * x, axis=-1, keepdims=True)
    return x * lax.rsqrt(ms + NORM_EPS) * g


def _sigmoid(x):
    return 1.0 / (1.0 + jnp.exp(-x))


def _silu(x):
    return x * _sigmoid(x)


def _softplus(x):
    return jnp.maximum(x, 0.0) + jnp.log1p(jnp.exp(-jnp.abs(x)))


def _gelu_tanh(x):
    c = 0.7978845608028654
    return 0.5 * x * (1.0 + jnp.tanh(c * (x + 0.044715 * (x * x * x))))


ROW_CHUNK = 16
ROW_UNROLL = 4


def _row_chunks(n_rows, body):
    def step(c, carry):
        body(pl.ds(pl.multiple_of(c * ROW_CHUNK, ROW_CHUNK), ROW_CHUNK))
        return carry
    lax.fori_loop(0, n_rows // ROW_CHUNK, step, 0, unroll=ROW_UNROLL)


def _mod_rows(ref, rows):
    return ref[...] if ref.shape[0] == 1 else ref[rows, :]


def _prenorm_rows(h_ref, x_ref, sh_ref, sc_ref, g_ref, rows):
    y = _rms(x_ref[rows, :], g_ref[...])
    h_ref[rows, :] = (y * (1.0 + _mod_rows(sc_ref, rows)) + _mod_rows(sh_ref, rows)).astype(BF16)


def _prenorm_into(h_ref, x_ref, sh_ref, sc_ref, g_ref):
    _row_chunks(x_ref.shape[0], functools.partial(_prenorm_rows, h_ref, x_ref, sh_ref, sc_ref, g_ref))


def _postnorm_into(o_ref, y_ref, x_ref, gt_ref, g_ref, coef, clear_y=False):
    def body(rows):
        o_ref[rows, :] = x_ref[rows, :] + (coef * _mod_rows(gt_ref, rows)) * _rms(y_ref[rows, :], g_ref[...])
        if clear_y:
            y_ref[rows, :] = jnp.zeros((ROW_CHUNK, y_ref.shape[1]), y_ref.dtype)
    _row_chunks(x_ref.shape[0], body)


class Stream:
    def __init__(self, x, mod, per_row, bm=None):
        self.x, self.mod, self.per_row = x, mod, per_row
        self.G, self.R, self.D = x.shape
        self.bm = min(512, self.R) if bm is None else bm

    def with_x(self, x):
        return Stream(x, self.mod, self.per_row)

    def rows(self, width):
        return pl.BlockSpec((None, self.bm, width), lambda g, i, *_: (g, i, 0))

    def mod_spec(self, li, k):
        G, per_row = self.G, self.per_row
        rm = self.bm if per_row else 1
        return pl.BlockSpec((None, rm, self.D), lambda g, i, *_: (li * G + g, i if per_row else 0, k))


def _gain_spec(idx, width):
    return pl.BlockSpec((None, 1, width), lambda *_: (idx, 0, 0))


def _ada_kernel(c_ref, w_ref, b_ref, o_ref):
    cs = _silu(c_ref[...]).astype(BF16)
    o_ref[...] = _dot(cs, w_ref[...].astype(BF16)) + b_ref[...]


def ada_mod(c_all, ada_w, ada_b):
    Bc, D = c_all.shape
    N = ada_w.shape[-1]
    bn = 1024
    return pl.pallas_call(
        _ada_kernel,
        grid=(DEPTH, N // bn),
        in_specs=[pl.BlockSpec((Bc, D), lambda l, j: (0, 0)),
                  pl.BlockSpec((None, D, bn), lambda l, j: (l, 0, j)),
                  pl.BlockSpec((None, 1, bn), lambda l, j: (l, 0, j))],
        out_specs=pl.BlockSpec((None, Bc, bn), lambda l, j: (l, 0, j)),
        out_shape=jax.ShapeDtypeStruct((DEPTH, Bc, N), F32),
        compiler_params=_cparams("parallel", "parallel"),
        name="ada_mod",
    )(c_all, ada_w, ada_b.reshape(DEPTH, 1, N))


FFN_BF = 256
FFN_BM = 1024


def _swiglu_step(h, wg, wu, wd):
    g = _dot(h, wg)
    u = _dot(h, wu)
    return _dot((_silu(g) * u).astype(BF16), wd)


def _ffn_emit_kernel(x_ref, sh_ref, sc_ref, gt_ref, gpre_ref, gpost_ref, wg_ref, wu_ref, wd_ref,
                     o_ref, wgo_ref, wuo_ref, wdo_ref, h_ref, acc_ref):
    j = pl.program_id(0)

    @pl.when(j == 0)
    def _():
        _prenorm_into(h_ref, x_ref, sh_ref, sc_ref, gpre_ref)
        acc_ref[...] = jnp.zeros_like(acc_ref)

    wg, wu, wd = wg_ref[...].astype(BF16), wu_ref[...].astype(BF16), wd_ref[...].astype(BF16)
    wgo_ref[...] = wg
    wuo_ref[...] = wu
    wdo_ref[...] = wd
    acc_ref[...] += _swiglu_step(h_ref[...], wg, wu, wd)

    @pl.when(j == pl.num_programs(0) - 1)
    def _():
        _postnorm_into(o_ref, acc_ref, x_ref, gt_ref, gpost_ref, FFN_RES)


def ffn_emit(s, li, sub, wg, wu, wd, norm_pre, norm_post):
    assert s.G == 1 and s.R == s.bm
    D, DF = s.D, wd.shape[-2]
    k0, bf = 3 * sub, FFN_BF
    nj = DF // bf
    row = lambda width: pl.BlockSpec((None, s.bm, width), lambda j: (0, 0, 0))
    mod = lambda k: pl.BlockSpec((None, s.bm, D), lambda j: (li, 0, k))
    out = pl.pallas_call(
        _ffn_emit_kernel,
        grid=(nj,),
        in_specs=[row(D), mod(k0), mod(k0 + 1), mod(k0 + 2), _gain_spec(li * 3 + sub, D), _gain_spec(li * 3 + sub, D),
                  pl.BlockSpec((None, D, bf), lambda j: (li, 0, j)), pl.BlockSpec((None, D, bf), lambda j: (li, 0, j)),
                  pl.BlockSpec((None, bf, D), lambda j: (li, j, 0))],
        out_specs=[row(D), pl.BlockSpec((None, D, bf), lambda j: (j, 0, 0)), pl.BlockSpec((None, D, bf), lambda j: (j, 0, 0)),
                   pl.BlockSpec((bf, D), lambda j: (j, 0))],
        out_shape=[jax.ShapeDtypeStruct(s.x.shape, F32), jax.ShapeDtypeStruct((nj, D, bf), BF16),
                   jax.ShapeDtypeStruct((nj, D, bf), BF16), jax.ShapeDtypeStruct((DF, D), BF16)],
        scratch_shapes=[pltpu.VMEM((s.bm, D), BF16), pltpu.VMEM((s.bm, D), F32)],
        compiler_params=_cparams("arbitrary"),
        name="ffn_emit",
    )(s.x, s.mod, s.mod, s.mod, norm_pre, norm_post, wg, wu, wd)
    return out[0], out[1:]


def _ffn_kernel(x_ref, sh_ref, sc_ref, gt_ref, gpre_ref, gpost_ref, wg_ref, wu_ref, wd_ref, o_ref, h_ref, acc_ref):
    j = pl.program_id(2)
    first_tile = (pl.program_id(0) == 0) & (pl.program_id(1) == 0)

    @pl.when(j == 0)
    def _():
        _prenorm_into(h_ref, x_ref, sh_ref, sc_ref, gpre_ref)

        @pl.when(first_tile)
        def _():
            acc_ref[...] = jnp.zeros_like(acc_ref)

    acc_ref[...] += _swiglu_step(h_ref[...], wg_ref[...], wu_ref[...], wd_ref[...])

    @pl.when(j == pl.num_programs(2) - 1)
    def _():
        _postnorm_into(o_ref, acc_ref, x_ref, gt_ref, gpost_ref, FFN_RES, clear_y=True)


def ffn(s, li, sub, wg, wu, wd, norm_pre, norm_post):
    assert not s.per_row
    G, D = s.G, s.D
    DF = wd.shape[0]
    k0, bf = 3 * sub, FFN_BF
    bm = min(FFN_BM, s.R)
    rows = pl.BlockSpec((None, bm, D), lambda g, i, j: (g, i, 0))
    mod = lambda k: pl.BlockSpec((None, 1, D), lambda g, i, j: (li * G + g, 0, k))
    return pl.pallas_call(
        _ffn_kernel,
        grid=(G, s.R // bm, DF // bf),
        in_specs=[rows, mod(k0), mod(k0 + 1), mod(k0 + 2), _gain_spec(li * 3 + sub, D), _gain_spec(li * 3 + sub, D),
                  pl.BlockSpec((None, D, bf), lambda g, i, j: (j, 0, 0)), pl.BlockSpec((None, D, bf), lambda g, i, j: (j, 0, 0)),
                  pl.BlockSpec((bf, D), lambda g, i, j: (j, 0))],
        out_specs=rows,
        out_shape=jax.ShapeDtypeStruct(s.x.shape, F32),
        scratch_shapes=[pltpu.VMEM((bm, D), BF16), pltpu.VMEM((bm, D), F32)],
        compiler_params=_cparams("arbitrary", "arbitrary", "arbitrary"),
        name="ffn",
    )(s.x, s.mod, s.mod, s.mod, norm_pre, norm_post, wg, wu, wd)


def _mm_post_kernel(a_ref, w_ref, x_ref, gt_ref, gpost_ref, o_ref):
    y = _dot(a_ref[...], w_ref[...])
    o_ref[...] = x_ref[...] + gt_ref[...] * _rms(y, gpost_ref[...])


def mm_post(s, li, a, w, norm_post):
    K, D = w.shape
    bm = min(256, s.R)
    rows = lambda width: pl.BlockSpec((None, bm, width), lambda g, i: (g, i, 0))
    G, per_row = s.G, s.per_row
    gate = pl.BlockSpec((None, bm if per_row else 1, D), lambda g, i: (li * G + g, i if per_row else 0, 5))
    return pl.pallas_call(
        _mm_post_kernel,
        grid=(s.G, s.R // bm),
        in_specs=[rows(K), pl.BlockSpec((K, D), lambda g, i: (0, 0)), rows(D), gate, _gain_spec(li * 3 + 1, D)],
        out_specs=rows(D),
        out_shape=jax.ShapeDtypeStruct(s.x.shape, F32),
        compiler_params=_cparams("parallel", "parallel"),
        name="mm_post",
    )(a, w, s.x, s.mod, norm_post)


def _mla_proj_kernel(x_ref, sh_ref, sc_ref, gpre_ref, w_ref, gq_ref, gkv_ref, cs_ref, ql_ref, kv_ref, kvb_ref):
    h = (_rms(x_ref[...], gpre_ref[...]) * (1.0 + sc_ref[...]) + sh_ref[...]).astype(BF16)
    r = _dot(h, w_ref[...])
    ql_ref[...] = _rms(r[:, :Q_LORA], gq_ref[...]).astype(BF16)
    ckv = _rms(r[:, Q_LORA:Q_LORA + KV_LORA], gkv_ref[...])
    t = r[:, Q_LORA + KV_LORA:] * cs_ref[...]
    kr = t[:, :QK_ROPE] + t[:, QK_ROPE:]
    kv_ref[:, :KV_LORA] = ckv
    kv_ref[:, KV_LORA:] = kr
    kvb_ref[:, :KV_LORA] = ckv.astype(BF16)
    kvb_ref[:, KV_LORA:] = jnp.concatenate([kr, jnp.zeros_like(kr)], axis=-1).astype(BF16)


def mla_proj(s, li, w_cat, g_q, g_kv, cs_tab, norm_pre):
    D, N = w_cat.shape
    out = (jax.ShapeDtypeStruct((s.G, s.R, Q_LORA), BF16),
           jax.ShapeDtypeStruct((s.G, s.R, KV_ROW), F32),
           jax.ShapeDtypeStruct((s.G, s.R, KV_PAD), BF16))
    return pl.pallas_call(
        _mla_proj_kernel,
        grid=(s.G, s.R // s.bm),
        in_specs=[s.rows(D), s.mod_spec(li, 3), s.mod_spec(li, 4), _gain_spec(li * 3 + 1, D),
                  pl.BlockSpec((D, N), lambda g, i: (0, 0)),
                  pl.BlockSpec((1, Q_LORA), lambda g, i: (0, 0)),
                  pl.BlockSpec((1, KV_LORA), lambda g, i: (0, 0)),
                  pl.BlockSpec((s.bm, LANE), lambda g, i: (i, 0))],
        out_specs=(s.rows(Q_LORA), s.rows(KV_ROW), s.rows(KV_PAD)),
        out_shape=out,
        compiler_params=_cparams("parallel", "parallel"),
        name="mla_proj",
    )(s.x, s.mod, s.mod, norm_pre, w_cat, g_q, g_kv, cs_tab)


def _q_expand_kernel(ql_ref, w_ref, wr_ref, cm_ref, sm_ref, q_ref, *, hb):
    ql = ql_ref[...]
    a = _dot(ql, w_ref[...])
    b = _dot(ql, wr_ref[...])
    cm, sm = cm_ref[...], sm_ref[...]
    for h in range(hb):
        lo = h * QK_PAD
        q_ref[:, lo:lo + LANE] = (a[:, lo:lo + LANE] * cm[:, :LANE]).astype(BF16)
        hi = a[:, lo + LANE:lo + QK_PAD] * cm[:, LANE:] + b[:, h * LANE:(h + 1) * LANE] * sm[:, LANE:]
        q_ref[:, lo + LANE:lo + QK_PAD] = hi.astype(BF16)


def q_expand(s, ql, w_q, w_qrot, cmap, smap):
    hb = 4
    nw = hb * QK_PAD
    wspec = pl.BlockSpec((Q_LORA, nw), lambda g, i, j: (0, j))
    wrspec = pl.BlockSpec((Q_LORA, hb * LANE), lambda g, i, j: (0, j))
    tspec = pl.BlockSpec((s.bm, QK_PAD), lambda g, i, j: (i, 0))
    return pl.pallas_call(
        functools.partial(_q_expand_kernel, hb=hb),
        grid=(s.G, s.R // s.bm, MLA_HEADS // hb),
        in_specs=[s.rows(Q_LORA), wspec, wrspec, tspec, tspec],
        out_specs=pl.BlockSpec((None, s.bm, nw), lambda g, i, j: (g, i, j)),
        out_shape=jax.ShapeDtypeStruct((s.G, s.R, MLA_HEADS * QK_PAD), BF16),
        compiler_params=_cparams("parallel", "parallel", "parallel"),
        name="q_expand",
    )(ql, w_q, w_qrot, cmap, smap)


def _kv_expand_kernel(kvb_ref, wk_ref, wv_ref, k_ref, v_ref):
    kvb = kvb_ref[...]
    k_ref[...] = _dot(kvb, wk_ref[...]).astype(BF16)
    v_ref[...] = _dot(kvb, wv_ref[...]).astype(BF16)


def kv_expand(s, kvb, w_k, w_v):
    hb = 4
    return pl.pallas_call(
        _kv_expand_kernel,
        grid=(s.G, s.R // s.bm, MLA_HEADS // hb),
        in_specs=[s.rows(KV_PAD),
                  pl.BlockSpec((KV_PAD, hb * QK_PAD), lambda g, i, j: (0, j)),
                  pl.BlockSpec((KV_PAD, hb * V_HEAD), lambda g, i, j: (0, j))],
        out_specs=(pl.BlockSpec((None, s.bm, hb * QK_PAD), lambda g, i, j: (g, i, j)),
                   pl.BlockSpec((None, s.bm, hb * V_HEAD), lambda g, i, j: (g, i, j))),
        out_shape=(jax.ShapeDtypeStruct((s.G, s.R, MLA_HEADS * QK_PAD), BF16),
                   jax.ShapeDtypeStruct((s.G, s.R, MLA_HEADS * V_HEAD), BF16)),
        compiler_params=_cparams("parallel", "parallel", "parallel"),
        name="kv_expand",
    )(kvb, w_k, w_v)


def _flash_kernel(q_ref, k_ref, v_ref, o_ref, m_ref, acc_ref, *, blk, hb):
    i = pl.program_id(2)
    m_ref[...] = jnp.full_like(m_ref, NEG_INF)
    acc_ref[...] = jnp.zeros_like(acc_ref)
    ones_col = jnp.where(lax.broadcasted_iota(jnp.int32, (blk, QK_PAD - V_HEAD), 1) == 0, 1.0, 0.0).astype(BF16)

    def block(j, diagonal):
        r0 = pl.multiple_of(j * blk, blk)
        for h in range(hb):
            hs = slice(h * QK_PAD, (h + 1) * QK_PAD)
            v_ext = jnp.concatenate([v_ref[pl.ds(r0, blk), h * V_HEAD:(h + 1) * V_HEAD], ones_col], axis=1)
            s = _dot_nt(q_ref[:, hs], k_ref[pl.ds(r0, blk), hs])
            if diagonal:
                row = lax.broadcasted_iota(jnp.int32, (blk, blk), 0)
                col = lax.broadcasted_iota(jnp.int32, (blk, blk), 1)
                s = jnp.where(col <= row, s, NEG_INF)
            m_prev = m_ref[h]
            m_new = jnp.maximum(m_prev, jnp.max(s, axis=-1, keepdims=True))
            alpha = jnp.exp(m_prev - m_new)
            p = jnp.exp(s - jnp.concatenate([m_new] * (blk // LANE), axis=1))
            pv = _dot(p.astype(BF16), v_ext)
            acc_ref[h] = jnp.concatenate([alpha] * (QK_PAD // LANE), axis=1) * acc_ref[h] + pv
            m_ref[h] = m_new

    def below_diagonal(jj, carry):
        block(2 * jj, False)
        block(2 * jj + 1, False)
        return carry

    lax.fori_loop(0, i // 2, below_diagonal, 0)

    @pl.when(i % 2 == 1)
    def _():
        block(i - 1, False)

    block(i, True)
    for h in range(hb):
        acc = acc_ref[h]
        o_ref[:, h * V_HEAD:(h + 1) * V_HEAD] = (acc[:, :V_HEAD] / acc[:, V_HEAD:V_HEAD + 1]).astype(o_ref.dtype)


def flash_prompt(q, k, v):
    B, L, _ = q.shape
    blk = min(512, L)
    hb = 4
    seq = lambda width: pl.BlockSpec((None, L, hb * width), lambda b, h, i: (b, 0, h))
    return pl.pallas_call(
        functools.partial(_flash_kernel, blk=blk, hb=hb),
        grid=(B, MLA_HEADS // hb, L // blk),
        in_specs=[pl.BlockSpec((None, blk, hb * QK_PAD), lambda b, h, i: (b, i, h)), seq(QK_PAD), seq(V_HEAD)],
        out_specs=pl.BlockSpec((None, blk, hb * V_HEAD), lambda b, h, i: (b, i, h)),
        out_shape=jax.ShapeDtypeStruct((B, L, MLA_HEADS * V_HEAD), BF16),
        scratch_shapes=[pltpu.VMEM((hb, blk, LANE), F32), pltpu.VMEM((hb, blk, QK_PAD), F32)],
        compiler_params=_cparams("parallel", "parallel", "parallel"),
        name="flash_prompt",
    )(q, k, v)


def _q_absorb_kernel(q_ref, w_ref, qc_ref):
    q = q_ref[...]
    qc_ref[:, :KV_LORA] = _dot(q[:, :QK_NOPE], w_ref[...]).astype(BF16)
    qc_ref[:, KV_LORA:] = q[:, QK_NOPE:]


def q_absorb(q, w_ukT):
    R = q.shape[0]
    return pl.pallas_call(
        _q_absorb_kernel,
        grid=(MLA_HEADS,),
        in_specs=[pl.BlockSpec((R, QK_PAD), lambda h: (0, h)),
                  pl.BlockSpec((None, QK_NOPE, KV_LORA), lambda h: (h, 0, 0))],
        out_specs=pl.BlockSpec((None, R, KV_PAD), lambda h: (h, 0, 0)),
        out_shape=jax.ShapeDtypeStruct((MLA_HEADS, R, KV_PAD), BF16),
        compiler_params=_cparams("parallel"),
        name="q_absorb",
    )(q, w_ukT)


PAGE_SLOTS = 3


def _paged_kernel(pt_ref, qc_ref, newt_ref, cache_ref, o_ref, buf_ref, sem_ref, kb_ref, kn_ref, m_ref, l_ref, acc_ref,
                  *, layer, pp, group, steps_per_seq, heads):
    t = pl.program_id(0)
    n_steps = pl.num_programs(0)
    j = t % steps_per_seq
    rows = qc_ref.shape[0]
    gw = group * PAGE_SIZE

    def page_copy(step, k):
        slot = step % PAGE_SLOTS
        page = pt_ref[step * pp + k]
        return pltpu.make_async_copy(cache_ref.at[layer, page], buf_ref.at[slot, k], sem_ref.at[slot])

    def fetch(step):
        for k in range(pp):
            page_copy(step, k).start()

    @pl.when(t == 0)
    def _():
        for ahead in range(PAGE_SLOTS - 1):
            @pl.when(ahead < n_steps)
            def _():
                fetch(ahead)

    @pl.when(t + (PAGE_SLOTS - 1) < n_steps)
    def _():
        fetch(t + (PAGE_SLOTS - 1))

    @pl.when(j == 0)
    def _():
        m_ref[...] = jnp.full_like(m_ref, NEG_INF)
        l_ref[...] = jnp.zeros_like(l_ref)
        acc_ref[...] = jnp.zeros_like(acc_ref)
        kb_ref[KV_ROW:, :] = jnp.zeros((KV_PAD - KV_ROW, kb_ref.shape[1]), BF16)

    def update(key_refs, visible):
        q = qc_ref[...]
        ss = [_dot(q, kr[...]) for kr in key_refs]
        if visible is not None:
            ss = [jnp.where(visible, s, NEG_INF) for s in ss]
        m_prev = m_ref[...]
        m_new = m_prev
        for s in ss:
            m_new = jnp.maximum(m_new, jnp.max(s, axis=-1, keepdims=True))
        alpha = jnp.exp(m_prev - m_new)
        l_new = alpha * l_ref[...]
        acc = alpha * acc_ref[...]
        for s, kr in zip(ss, key_refs):
            p = jnp.exp(s - m_new)
            l_new = l_new + jnp.sum(p, axis=-1, keepdims=True)
            acc = acc + _dot_nt(p.astype(BF16), kr[:KV_LORA, :])
        l_ref[...] = l_new
        acc_ref[...] = acc
        m_ref[...] = m_new

    slot = t % PAGE_SLOTS
    for k in range(pp):
        page_copy(t, k).wait()
    for k in range(pp):
        kb_ref[:KV_ROW, k * PAGE_SIZE:(k + 1) * PAGE_SIZE] = buf_ref[slot, k].astype(BF16)
    update([kb_ref.at[:, g * gw:(g + 1) * gw] for g in range(pp // group)], None)

    @pl.when(j == steps_per_seq - 1)
    def _():
        T = newt_ref.shape[1]
        kn_ref[...] = jnp.zeros_like(kn_ref)
        kn_ref[:KV_ROW, :T] = newt_ref[...].astype(BF16)
        row = lax.broadcasted_iota(jnp.int32, (rows, PAGE_SIZE), 0)
        col = lax.broadcasted_iota(jnp.int32, (rows, PAGE_SIZE), 1)
        update([kn_ref], col * heads <= row)
        o_ref[...] = acc_ref[...] / l_ref[...]


def paged_attention(qc, kv_new, cache_t, layer, page_table):
    Bd, rows, _ = qc.shape
    T = kv_new.shape[1]
    n_pages = page_table.shape[1]
    pp = 32 if n_pages % 32 == 0 else n_pages
    group = 8 if pp % 8 == 0 else pp
    sps = n_pages // pp
    grid_spec = pltpu.PrefetchScalarGridSpec(
        num_scalar_prefetch=1,
        grid=(Bd * sps,),
        in_specs=[pl.BlockSpec((None, rows, KV_PAD), lambda t, pt: (t // sps, 0, 0)),
                  pl.BlockSpec((None, KV_ROW, T), lambda t, pt: (t // sps, 0, 0)),
                  pl.BlockSpec(memory_space=pl.ANY)],
        out_specs=pl.BlockSpec((None, rows, KV_LORA), lambda t, pt: (t // sps, 0, 0)),
        scratch_shapes=[pltpu.VMEM((PAGE_SLOTS, pp, KV_ROW, PAGE_SIZE), F32), pltpu.SemaphoreType.DMA((PAGE_SLOTS,)),
                        pltpu.VMEM((KV_PAD, pp * PAGE_SIZE), BF16), pltpu.VMEM((KV_PAD, PAGE_SIZE), BF16),
                        pltpu.VMEM((rows, 1), F32), pltpu.VMEM((rows, 1), F32), pltpu.VMEM((rows, KV_LORA), F32)])
    return pl.pallas_call(
        functools.partial(_paged_kernel, layer=layer, pp=pp, group=group, steps_per_seq=sps, heads=rows // T),
        grid_spec=grid_spec,
        out_shape=jax.ShapeDtypeStruct((Bd, rows, KV_LORA), F32),
        compiler_params=_cparams("arbitrary"),
        name="paged_attention",
    )(page_table.reshape(-1), qc, kv_new.transpose(0, 2, 1), cache_t)


def _v_up_kernel(o_ref, w_ref, y_ref):
    y_ref[...] = _dot(o_ref[...].astype(BF16), w_ref[...]).astype(BF16)


def v_up(o_lat, w_uv):
    R = o_lat.shape[0]
    return pl.pallas_call(
        _v_up_kernel,
        grid=(MLA_HEADS,),
        in_specs=[pl.BlockSpec((R, KV_LORA), lambda h: (0, h)), pl.BlockSpec((KV_LORA, V_HEAD), lambda h: (0, h))],
        out_specs=pl.BlockSpec((R, V_HEAD), lambda h: (0, h)),
        out_shape=jax.ShapeDtypeStruct((R, MLA_HEADS * V_HEAD), BF16),
        compiler_params=_cparams("parallel"),
        name="v_up",
    )(o_lat, w_uv)


def _hg_proj_kernel(x_ref, sh_ref, sc_ref, gpre_ref, wq_ref, wf_ref, wi_ref, wg_ref, lb_ref,
                    q_ref, k_ref, v_ref, lf_ref, g_ref, h_ref):
    @pl.when(pl.program_id(2) == 0)
    def _():
        _prenorm_into(h_ref, x_ref, sh_ref, sc_ref, gpre_ref)

    h = h_ref[...]
    q_ref[...] = _silu(_dot(h, wq_ref[...])).astype(q_ref.dtype)
    fz = _dot(h, wf_ref[...])
    lb = lb_ref[...]
    a = jnp.log(lb)
    b = jnp.log1p(-lb) + (jnp.minimum(fz, 0.0) - jnp.log1p(jnp.exp(-jnp.abs(fz))))
    lf_ref[...] = jnp.maximum(a, b) + jnp.log1p(jnp.exp(-jnp.abs(a - b)))
    k_ref[...] = ((1.0 - lb) * _sigmoid(-fz)).astype(k_ref.dtype)
    v_ref[...] = _dot(h, wi_ref[...]).astype(v_ref.dtype)
    g_ref[...] = _silu(_dot(h, wg_ref[...])).astype(g_ref.dtype)


PROJ_BM = 1024
PROJ_BN = 256


def _blocked(w):
    D, N = w.shape
    return w.reshape(D, N // PROJ_BN, PROJ_BN).transpose(1, 0, 2).astype(BF16)


def hg_proj(s, li, w_q, w_f, w_i, w_g, lb, norm_pre):
    s = Stream(s.x, s.mod, s.per_row, bm=min(PROJ_BM, s.R))
    nb, D, bn = w_q.shape
    wspec = pl.BlockSpec((None, D, bn), lambda g, i, j: (j, 0, 0))
    ospec = pl.BlockSpec((None, s.bm, bn), lambda g, i, j: (g, i, j))
    shape = lambda dt: jax.ShapeDtypeStruct((s.G, s.R, nb * bn), dt)
    return pl.pallas_call(
        _hg_proj_kernel,
        grid=(s.G, s.R // s.bm, nb),
        in_specs=[s.rows(D), s.mod_spec(li, 3), s.mod_spec(li, 4), _gain_spec(li * 3 + 1, D),
                  wspec, wspec, wspec, wspec, pl.BlockSpec((1, bn), lambda g, i, j: (0, j))],
        out_specs=(ospec,) * 5,
        out_shape=(shape(BF16), shape(BF16), shape(BF16), shape(F32), shape(BF16)),
        scratch_shapes=[pltpu.VMEM((s.bm, D), BF16)],
        compiler_params=_cparams("parallel", "parallel", "arbitrary"),
        name="hg_proj",
    )(s.x, s.mod, s.mod, norm_pre, w_q, w_f, w_i, w_g, lb)


def _cumsum_rows(x, span):
    row = lax.broadcasted_iota(jnp.int32, x.shape, 0)
    pos = jnp.bitwise_and(row, span - 1)
    sh = 1
    while sh < span:
        x = x + jnp.where(pos >= sh, pltpu.roll(x, sh, 0), 0.0)
        sh *= 2
    return x


def _chunk_id(r):
    return lax.shift_right_logical(r, HG_CHUNK.bit_length() - 1)


def _hg_scan_kernel(q_ref, k_ref, v_ref, lf_ref, g_ref, s0_ref, gn_ref, o_ref, so_ref, st_ref, *, hb, chained):
    n = HG_BLOCK // HG_CHUNK
    first = pl.program_id(2) == 0
    last = pl.program_id(2) == pl.num_programs(2) - 1

    if chained:
        @pl.when(first)
        def _():
            for h in range(hb):
                st_ref[h] = s0_ref[h].T

    row = lax.broadcasted_iota(jnp.int32, (HG_BLOCK, HG_BLOCK), 0)
    col = lax.broadcasted_iota(jnp.int32, (HG_BLOCK, HG_BLOCK), 1)
    causal = (_chunk_id(row) == _chunk_id(col)) & (col <= row)
    chunk_of_row = _chunk_id(lax.broadcasted_iota(jnp.int32, (HG_BLOCK, HG_DK), 0))

    for h in range(hb):
        sl = slice(h * HG_DK, (h + 1) * HG_DK)
        b = _cumsum_rows(lf_ref[:, sl], HG_CHUNK)
        b_last = jnp.concatenate(
            [jnp.broadcast_to(b[c * HG_CHUNK + HG_CHUNK - 1:(c + 1) * HG_CHUNK, :], (HG_CHUNK, HG_DK)) for c in range(n)],
            axis=0)
        q, k, v = q_ref[:, sl].astype(F32), k_ref[:, sl].astype(F32), v_ref[:, sl].astype(F32)
        qg = q * jnp.exp(b)
        kd = (k * jnp.exp(-b)).astype(BF16)
        kl = k * jnp.exp(b_last - b)
        qgb = qg.astype(BF16)
        vb = v.astype(BF16)
        a = jnp.where(causal, _dot_nt(qgb, kd), 0.0)
        o = _dot(a.astype(BF16), vb)
        klx = jnp.concatenate([jnp.where(chunk_of_row == c, kl, 0.0).astype(BF16) for c in range(n)], axis=1)
        qx = jnp.concatenate([jnp.where(chunk_of_row == c, qg, 0.0).astype(BF16) for c in range(n)], axis=1)
        if chained:
            ut = _dot(v.T.astype(BF16), klx)
            sts = []
            st = st_ref[h]
            for c in range(n):
                sts.append(st.astype(BF16))
                dec = jnp.exp(b_last[c * HG_CHUNK:c * HG_CHUNK + 1, :])
                st = st * dec + ut[:, c * HG_DK:(c + 1) * HG_DK]
            st_ref[h] = st

            @pl.when(last)
            def _():
                so_ref[h] = st.T
            o = o + _dot_nt(qx, jnp.concatenate(sts, axis=1))
        else:
            s0 = [s0_ref[c, h] for c in range(n)]
            dec_rows = jnp.concatenate([b_last[c * HG_CHUNK:c * HG_CHUNK + 1, :] for c in range(n)], axis=0)
            dec_cols = jnp.exp(dec_rows).T
            u_all = lax.dot_general(klx, vb, (((0,), (0,)), ((), ())), preferred_element_type=F32)
            for c in range(n):
                so_ref[c, h] = s0[c] * dec_cols[:, c:c + 1] + u_all[c * HG_DK:(c + 1) * HG_DK, :]
            o = o + _dot(qx, jnp.concatenate([s.astype(BF16) for s in s0], axis=0))
        o_ref[:, sl] = (_rms(o, gn_ref[...]) * g_ref[:, sl]).astype(BF16)


def hg_scan(q, k, v, lf, gate, s0, g_norm, chained):
    G, R, N = q.shape
    hb = 8 if chained else 4
    n = HG_BLOCK // HG_CHUNK
    rows = pl.BlockSpec((None, HG_BLOCK, hb * HG_DK), lambda g, h, l: (g, l, h))
    if chained:
        sspec = pl.BlockSpec((None, hb, HG_DK, HG_DV), lambda g, h, l: (g, h, 0, 0))
    else:
        sspec = pl.BlockSpec((None, n, hb, HG_DK, HG_DV), lambda g, h, l: (g, 0, h, 0, 0))
    return pl.pallas_call(
        functools.partial(_hg_scan_kernel, hb=hb, chained=chained),
        grid=(G, HG_HEADS // hb, R // HG_BLOCK),
        in_specs=[rows, rows, rows, rows, rows, sspec, pl.BlockSpec((1, HG_DV), lambda g, h, l: (0, 0))],
        out_specs=(rows, sspec),
        out_shape=(jax.ShapeDtypeStruct((G, R, N), BF16), jax.ShapeDtypeStruct(s0.shape, F32)),
        scratch_shapes=[pltpu.VMEM((hb, HG_DV, HG_DK), F32)],
        compiler_params=_cparams("parallel", "parallel", "arbitrary"),
        name="hg_scan",
    )(q, k, v, lf, gate, s0, g_norm)


def _lru_proj_kernel(x_ref, sh_ref, sc_ref, gpre_ref, wx_ref, wy_ref, u_ref, g_ref, h_ref):
    @pl.when(pl.program_id(2) == 0)
    def _():
        _prenorm_into(h_ref, x_ref, sh_ref, sc_ref, gpre_ref)

    h = h_ref[...]
    u_ref[...] = _dot(h, wx_ref[...])
    g_ref[...] = _gelu_tanh(_dot(h, wy_ref[...])).astype(g_ref.dtype)


def lru_proj(s, li, w_x, w_y, norm_pre):
    s = Stream(s.x, s.mod, s.per_row, bm=min(PROJ_BM, s.R))
    nb, D, bn = w_x.shape
    wspec = pl.BlockSpec((None, D, bn), lambda g, i, j: (j, 0, 0))
    ospec = pl.BlockSpec((None, s.bm, bn), lambda g, i, j: (g, i, j))
    return pl.pallas_call(
        _lru_proj_kernel,
        grid=(s.G, s.R // s.bm, nb),
        in_specs=[s.rows(D), s.mod_spec(li, 3), s.mod_spec(li, 4), _gain_spec(li * 3 + 1, D), wspec, wspec],
        out_specs=(ospec, ospec),
        out_shape=(jax.ShapeDtypeStruct((s.G, s.R, nb * bn), F32), jax.ShapeDtypeStruct((s.G, s.R, nb * bn), BF16)),
        scratch_shapes=[pltpu.VMEM((s.bm, D), BF16)],
        compiler_params=_cparams("parallel", "parallel", "arbitrary"),
        name="lru_proj",
    )(s.x, s.mod, s.mod, norm_pre, w_x, w_y)


def _lru_gates(xc, n, wra_ref, bra_ref, wix_ref, bix_ref, lam_ref):
    sl = slice(n * LRU_BW, (n + 1) * LRU_BW)
    xb = xc.astype(BF16)
    r = _sigmoid(_dot(xb, wra_ref[n]) + bra_ref[:, sl])
    ig = _sigmoid(_dot(xb, wix_ref[n]) + bix_ref[:, sl])
    log_a = -LRU_C * r * _softplus(-lam_ref[:, sl])
    a = jnp.exp(log_a)
    return a, jnp.sqrt(-jnp.tanh(log_a) * (a * a + 1.0)) * (ig * xc)


def _lru_scan_kernel(u_ref, gate_ref, cb_ref, h0_ref, cw_ref, cbias_ref, wra_ref, bra_ref, wix_ref, bix_ref, lam_ref,
                     yg_ref, hl_ref, cbo_ref, ext_ref, a_ref, b_ref, hc_ref, *, bl):
    l = pl.program_id(1)
    keep = CONV_W - 1
    base = 8

    @pl.when(l == 0)
    def _():
        ext_ref[base - keep:base, :] = cb_ref[...]
        hc_ref[...] = h0_ref[...]

    ext_ref[base:base + bl, :] = u_ref[...]
    for n in range(LRU_BLOCKS):
        sl = slice(n * LRU_BW, (n + 1) * LRU_BW)
        xc = cbias_ref[:, sl]
        for j in range(CONV_W):
            xc = xc + ext_ref[base - keep + j:base - keep + j + bl, sl] * cw_ref[j:j + 1, sl]
        a, b = _lru_gates(xc, n, wra_ref, bra_ref, wix_ref, bix_ref, lam_ref)
        a_ref[:, sl] = a
        b_ref[:, sl] = b

    pos = lax.broadcasted_iota(jnp.int32, (8, a_ref.shape[1]), 0)

    def tile(t, h_prev):
        r0 = pl.multiple_of(t * 8, 8)
        a_t = a_ref[pl.ds(r0, 8), :]
        b_t = b_ref[pl.ds(r0, 8), :]
        for sh in (1, 2, 4):
            m = pos >= sh
            b_t = jnp.where(m, a_t * pltpu.roll(b_t, sh, 0) + b_t, b_t)
            a_t = jnp.where(m, a_t * pltpu.roll(a_t, sh, 0), a_t)
        h_t = b_t + a_t * h_prev
        b_ref[pl.ds(r0, 8), :] = h_t
        return h_t[7:8, :]

    h_last = lax.fori_loop(0, bl // 8, tile, hc_ref[...])
    hc_ref[...] = h_last
    yg_ref[...] = (b_ref[...] * gate_ref[...]).astype(BF16)
    tail = ext_ref[base + bl - keep:base + bl, :]
    ext_ref[base - keep:base, :] = tail

    @pl.when(l == pl.num_programs(1) - 1)
    def _():
        hl_ref[...] = h_last
        cbo_ref[...] = tail


def lru_scan(u, gate, conv_buf, h0, conv_w, conv_b, w_ra, b_ra, w_ix, b_ix, lam):
    B, L, W = u.shape
    bl = min(256, L)
    rows = pl.BlockSpec((None, bl, W), lambda b, l: (b, l, 0))
    vec = pl.BlockSpec((1, W), lambda b, l: (0, 0))
    wblk = pl.BlockSpec((LRU_BLOCKS, LRU_BW, LRU_BW), lambda b, l: (0, 0, 0))
    hspec = pl.BlockSpec((None, 1, W), lambda b, l: (b, 0, 0))
    cspec = pl.BlockSpec((None, CONV_W - 1, W), lambda b, l: (b, 0, 0))
    return pl.pallas_call(
        functools.partial(_lru_scan_kernel, bl=bl),
        grid=(B, L // bl),
        in_specs=[rows, rows, cspec, hspec, pl.BlockSpec((CONV_W, W), lambda b, l: (0, 0)), vec,
                  wblk, vec, wblk, vec, vec],
        out_specs=(rows, hspec, cspec),
        out_shape=(jax.ShapeDtypeStruct((B, L, W), BF16), jax.ShapeDtypeStruct((B, 1, W), F32),
                   jax.ShapeDtypeStruct((B, CONV_W - 1, W), F32)),
        scratch_shapes=[pltpu.VMEM((bl + 8, W), F32), pltpu.VMEM((bl, W), F32), pltpu.VMEM((bl, W), F32),
                        pltpu.VMEM((1, W), F32)],
        compiler_params=_cparams("parallel", "arbitrary"),
        name="lru_scan",
    )(u, gate, conv_buf, h0.reshape(B, 1, W), conv_w, conv_b, w_ra, b_ra, w_ix, b_ix, lam)


def _lru_step_kernel(u_ref, gate_ref, cb_ref, h0_ref, cw_ref, cbias_ref, wra_ref, bra_ref, wix_ref, bix_ref, lam_ref,
                     yg_ref, hl_ref, cbo_ref):
    T = u_ref.shape[0]
    keep = CONV_W - 1
    for n in range(LRU_BLOCKS):
        sl = slice(n * LRU_BW, (n + 1) * LRU_BW)
        ext = [cb_ref[j, :, sl] for j in range(keep)] + [u_ref[t, :, sl] for t in range(T)]
        h = h0_ref[:, sl]
        for t in range(T):
            xc = cbias_ref[:, sl]
            for j in range(CONV_W):
                xc = xc + ext[t + j] * cw_ref[j:j + 1, sl]
            a, b = _lru_gates(xc, n, wra_ref, bra_ref, wix_ref, bix_ref, lam_ref)
            h = a * h + b
            yg_ref[t, :, sl] = (h * gate_ref[t, :, sl]).astype(BF16)
        hl_ref[:, sl] = h
        for j in range(keep):
            cbo_ref[j, :, sl] = ext[T + j]


def lru_step(u, gate, conv_buf, h0, conv_w, conv_b, w_ra, b_ra, w_ix, b_ix, lam):
    T, Bd, W = u.shape
    full = lambda *shape: pl.BlockSpec(shape, lambda i: (0,) * len(shape))
    return pl.pallas_call(
        _lru_step_kernel,
        grid=(1,),
        in_specs=[full(T, Bd, W), full(T, Bd, W), full(CONV_W - 1, Bd, W), full(Bd, W), full(CONV_W, W), full(1, W),
                  full(LRU_BLOCKS, LRU_BW, LRU_BW), full(1, W), full(LRU_BLOCKS, LRU_BW, LRU_BW), full(1, W), full(1, W)],
        out_specs=(full(T, Bd, W), full(Bd, W), full(CONV_W - 1, Bd, W)),
        out_shape=(jax.ShapeDtypeStruct((T, Bd, W), BF16), jax.ShapeDtypeStruct((Bd, W), F32),
                   jax.ShapeDtypeStruct((CONV_W - 1, Bd, W), F32)),
        compiler_params=_cparams("arbitrary"),
        name="lru_step",
    )(u, gate, conv_buf, h0, conv_w, conv_b, w_ra, b_ra, w_ix, b_ix, lam)


def _rope_tables(pos):
    half = QK_ROPE // 2
    inv = ROPE_THETA ** (-jnp.arange(half, dtype=F32) / half)
    ang = pos.astype(F32)[:, None] * inv[None, :]
    cos, sin = jnp.cos(ang), jnp.sin(ang)
    cos2, sin2 = jnp.concatenate([cos, cos], -1), jnp.concatenate([sin, sin], -1)
    ones, zeros = jnp.ones((pos.shape[0], QK_NOPE), F32), jnp.zeros((pos.shape[0], QK_NOPE), F32)
    pad = jnp.zeros((pos.shape[0], QK_PAD - QK_NOPE - QK_ROPE), F32)
    cs_tab = jnp.concatenate([cos2, sin2], -1)
    cmap = jnp.concatenate([ones, cos2, pad], -1) * MLA_SCALE
    smap = jnp.concatenate([zeros, sin2, pad], -1) * MLA_SCALE
    return cs_tab, cmap, smap


def _rot_cols(w):
    half = w.shape[-1] // 2
    return jnp.concatenate([-w[..., half:], w[..., :half]], -1)


def _mla_weights(w_dq, w_uq, w_dkv, w_uk, w_uv):
    kr = w_dkv[:, KV_LORA:]
    w_cat = jnp.concatenate([w_dq, w_dkv, _rot_cols(kr)], axis=1).astype(BF16)
    uq = w_uq.reshape(Q_LORA, MLA_HEADS, QK_NOPE + QK_ROPE)
    nope, rope = uq[..., :QK_NOPE], uq[..., QK_NOPE:]
    pad = jnp.zeros((Q_LORA, MLA_HEADS, QK_PAD - QK_NOPE - QK_ROPE), F32)
    w_q = jnp.concatenate([nope, rope, pad], -1).reshape(Q_LORA, MLA_HEADS * QK_PAD).astype(BF16)
    w_qrot = jnp.concatenate([_rot_cols(rope), pad], -1).reshape(Q_LORA, MLA_HEADS * LANE).astype(BF16)
    w_k = jnp.zeros((KV_PAD, MLA_HEADS, QK_PAD), F32)
    w_k = w_k.at[:KV_LORA, :, :QK_NOPE].set(w_uk)
    eye = jnp.broadcast_to(jnp.eye(QK_ROPE, dtype=F32)[:, None, :], (QK_ROPE, MLA_HEADS, QK_ROPE))
    w_k = w_k.at[KV_LORA:KV_ROW, :, QK_NOPE:QK_NOPE + QK_ROPE].set(eye)
    w_k = w_k.reshape(KV_PAD, MLA_HEADS * QK_PAD).astype(BF16)
    w_v = jnp.zeros((KV_PAD, MLA_HEADS * V_HEAD), F32).at[:KV_LORA].set(w_uv.reshape(KV_LORA, -1)).astype(BF16)
    w_ukT = w_uk.transpose(1, 2, 0).astype(BF16)
    w_uv2 = w_uv.reshape(KV_LORA, MLA_HEADS * V_HEAD).astype(BF16)
    return w_cat, w_q, w_qrot, w_k, w_v, w_ukT, w_uv2


def _hg_lower_bounds(lb_param):
    cs = jnp.cumsum(jax.nn.softmax(lb_param.astype(F32), axis=0), axis=0)
    return cs - cs[0]


def kernel(x_prompt, x_sample, c_prompt, c_sample, cache_mla_kv, page_table, state_hgrn, state_lru_h, state_lru_conv, ada_w, ada_b, norm_pre, norm_post, ffn1_wg, ffn1_wu, ffn1_wd, ffn2_wg, ffn2_wu, ffn2_wd, mla_w_dq, mla_g_q, mla_w_uq, mla_w_dkv, mla_g_kv, mla_w_uk, mla_w_uv, mla_w_o, hg_lb, hg_w_q, hg_w_f, hg_w_i, hg_w_g, hg_g_norm, hg_w_o, lru_w_x, lru_w_y, lru_conv_w, lru_conv_b, lru_w_ra, lru_b_ra, lru_w_ix, lru_b_ix, lru_lam, lru_w_o):
    B, L, D = x_prompt.shape
    Bd, T, _ = x_sample.shape
    n_pages = page_table.shape[1]
    past_len = n_pages * PAGE_SIZE

    n_s = Bd * T
    c_all = jnp.concatenate([jnp.repeat(c_sample, T, axis=0), c_prompt, jnp.zeros((-(n_s + B) % 8, D), F32)], axis=0)
    mod = ada_mod(c_all, ada_w, ada_b)
    mod_p = mod[:, n_s:n_s + B].reshape(DEPTH * B, 1, N_MOD * D)
    sp = Stream(x_prompt, mod_p, per_row=False)
    ss = Stream(x_sample.reshape(1, n_s, D), mod, per_row=True)

    npre = norm_pre.reshape(DEPTH * 3, 1, D)
    npost = norm_post.reshape(DEPTH * 3, 1, D)
    bf = lambda w: w.astype(BF16)
    f1 = (ffn1_wg, ffn1_wu, ffn1_wd)
    f2 = (ffn2_wg, ffn2_wu, ffn2_wd)
    cache_t = jnp.swapaxes(cache_mla_kv, 2, 3)

    tab_p = _rope_tables(jnp.arange(L))
    tab_s = _rope_tables(jnp.tile(past_len + jnp.arange(T), Bd))
    lbs = _hg_lower_bounds(hg_lb)

    kv_p, kv_s, hg_p, hg_s, lh_p, lh_s, lc_p, lc_s = [], [], [], [], [], [], [], []
    for li in range(DEPTH):
        xs, w_bf = ffn_emit(ss, li, 0, *f1, npre, npost)
        ss = ss.with_x(xs)
        sp = sp.with_x(ffn(sp, li, 0, *w_bf, npre, npost))
        kind, j = li % N_MIXERS, li // N_MIXERS
        if kind == 0:
            w_cat, w_q, w_qrot, w_k, w_v, w_ukT, w_uv2 = _mla_weights(
                mla_w_dq[j], mla_w_uq[j], mla_w_dkv[j], mla_w_uk[j], mla_w_uv[j])
            g_q, g_kv = mla_g_q[j].reshape(1, Q_LORA), mla_g_kv[j].reshape(1, KV_LORA)
            w_o = bf(mla_w_o[j])
            ql, kvr, kvb = mla_proj(sp, li, w_cat, g_q, g_kv, tab_p[0], npre)
            q = q_expand(sp, ql, w_q, w_qrot, tab_p[1], tab_p[2])
            kk, vv = kv_expand(sp, kvb, w_k, w_v)
            o = flash_prompt(q, kk, vv)
            sp = sp.with_x(mm_post(sp, li, o, w_o, npost))
            kv_p.append(kvr)
            ql, kvr, _ = mla_proj(ss, li, w_cat, g_q, g_kv, tab_s[0], npre)
            q = q_expand(ss, ql, w_q, w_qrot, tab_s[1], tab_s[2])
            qc = q_absorb(q[0], w_ukT)
            qc = qc.reshape(MLA_HEADS, Bd, T, KV_PAD).transpose(1, 2, 0, 3).reshape(Bd, T * MLA_HEADS, KV_PAD)
            kv_new = kvr.reshape(Bd, T, KV_ROW)
            o_lat = paged_attention(qc, kv_new, cache_t, j, page_table)
            o = v_up(o_lat.reshape(Bd * T, MLA_HEADS * KV_LORA), w_uv2)
            ss = ss.with_x(mm_post(ss, li, o.reshape(1, Bd * T, -1), w_o, npost))
            kv_s.append(kv_new)
        elif kind == 1:
            lb = lbs[li].reshape(1, -1)
            ws = (_blocked(hg_w_q[j]), _blocked(hg_w_f[j]), _blocked(hg_w_i[j]), _blocked(hg_w_g[j]))
            g_n = hg_g_norm[j].reshape(1, HG_DV)
            w_o = bf(hg_w_o[j])
            q, k, v, lf, gate = hg_proj(sp, li, *ws, lb, npre)
            s0 = jnp.zeros((B, HG_HEADS, HG_DK, HG_DV), F32)
            o, s_new = hg_scan(q, k, v, lf, gate, s0, g_n, chained=True)
            sp = sp.with_x(mm_post(sp, li, o, w_o, npost))
            hg_p.append(s_new)
            n_seq = HG_BLOCK // HG_CHUNK
            def to_chunks(a):
                a = jnp.pad(a.reshape(Bd, T, -1), ((0, 0), (HG_CHUNK - T, 0), (0, 0)))
                return a.reshape(Bd // n_seq, HG_BLOCK, -1)
            qs, ks, vs, lfs, gs = [to_chunks(a) for a in hg_proj(ss, li, *ws, lb, npre)]
            s0 = state_hgrn[j].reshape(Bd // n_seq, n_seq, HG_HEADS, HG_DK, HG_DV)
            o, s_new = hg_scan(qs, ks, vs, lfs, gs, s0, g_n, chained=False)
            o = o.reshape(Bd, HG_CHUNK, -1)[:, HG_CHUNK - T:].reshape(1, Bd * T, -1)
            ss = ss.with_x(mm_post(ss, li, o, w_o, npost))
            hg_s.append(s_new.reshape(Bd, HG_HEADS, HG_DK, HG_DV))
        else:
            w_x, w_y, w_o = _blocked(lru_w_x[j]), _blocked(lru_w_y[j]), bf(lru_w_o[j])
            W = w_o.shape[0]
            prm = (lru_conv_w[j], lru_conv_b[j].reshape(1, W), bf(lru_w_ra[j]), lru_b_ra[j].reshape(1, W),
                   bf(lru_w_ix[j]), lru_b_ix[j].reshape(1, W), lru_lam[j].reshape(1, W))
            u, gate = lru_proj(sp, li, w_x, w_y, npre)
            yg, h_last, cb = lru_scan(u, gate, jnp.zeros((B, CONV_W - 1, W), F32), jnp.zeros((B, W), F32), *prm)
            sp = sp.with_x(mm_post(sp, li, yg, w_o, npost))
            lh_p.append(h_last.reshape(B, W))
            lc_p.append(cb)
            u, gate = lru_proj(ss, li, w_x, w_y, npre)
            tm = lambda a: a.reshape(Bd, T, W).transpose(1, 0, 2)
            yg, h_last, cb = lru_step(tm(u), tm(gate), state_lru_conv[j].transpose(1, 0, 2), state_lru_h[j], *prm)
            ss = ss.with_x(mm_post(ss, li, yg.transpose(1, 0, 2).reshape(1, Bd * T, W), w_o, npost))
            lh_s.append(h_last)
            lc_s.append(cb.transpose(1, 0, 2))
        xs, w_bf = ffn_emit(ss, li, 2, *f2, npre, npost)
        ss = ss.with_x(xs)
        sp = sp.with_x(ffn(sp, li, 2, *w_bf, npre, npost))

    return (sp.x, ss.x.reshape(Bd, T, D), jnp.stack(kv_p), jnp.stack(kv_s), jnp.stack(hg_p), jnp.stack(hg_s),
            jnp.stack(lh_p), jnp.stack(lh_s), jnp.stack(lc_p), jnp.stack(lc_s))
```

```python
import functools

import jax
import jax.numpy as jnp
from jax import lax
from jax.experimental import pallas as pl
from jax.experimental.pallas import tpu as pltpu

F32 = jnp.float32
BF16 = jnp.bfloat16

DEPTH = 4
N_MIXERS = 3
NORM_EPS = 1e-6
N_MOD = 9
FFN_RES = 0.5

MLA_HEADS = 16
Q_LORA = 512
KV_LORA = 512
QK_NOPE = 128
QK_ROPE = 64
V_HEAD = 128
KV_ROW = KV_LORA + QK_ROPE
KV_PAD = 640
QK_PAD = 256
MLA_SCALE = (QK_NOPE + QK_ROPE) ** -0.5
ROPE_THETA = 10000.0
PAGE_SIZE = 128
NEG_INF = -1e30

HG_HEADS = 16
HG_DK = 128
HG_DV = 128
HG_CHUNK = 16
HG_BLOCK = 128

LRU_BLOCKS = 16
LRU_BW = 128
CONV_W = 4
LRU_C = 8.0

LANE = 128
VMEM_LIMIT = 56 * 1024 * 1024


def _cparams(*sem):
    return pltpu.CompilerParams(dimension_semantics=sem, vmem_limit_bytes=VMEM_LIMIT)


def _dot(a, b):
    return jnp.dot(a, b, preferred_element_type=F32)


def _dot_nt(a, b):
    return lax.dot_general(a, b, (((1,), (1,)), ((), ())), preferred_element_type=F32)


def _rms(x, g):
    ms = jnp.mean(x * x, axis=-1, keepdims=True)
    return x * lax.rsqrt(ms + NORM_EPS) * g


def _sigmoid(x):
    return 1.0 / (1.0 + jnp.exp(-x))


def _silu(x):
    return x * _sigmoid(x)


def _softplus(x):
    return jnp.maximum(x, 0.0) + jnp.log1p(jnp.exp(-jnp.abs(x)))


def _gelu_tanh(x):
    c = 0.7978845608028654
    return 0.5 * x * (1.0 + jnp.tanh(c * (x + 0.044715 * (x * x * x))))


ROW_CHUNK = 16
ROW_UNROLL = 4


def _row_chunks(n_rows, body):
    def step(c, carry):
        body(pl.ds(pl.multiple_of(c * ROW_CHUNK, ROW_CHUNK), ROW_CHUNK))
        return carry
    lax.fori_loop(0, n_rows // ROW_CHUNK, step, 0, unroll=ROW_UNROLL)


def _mod_rows(ref, rows):
    return ref[...] if ref.shape[0] == 1 else ref[rows, :]


def _prenorm_rows(h_ref, x_ref, sh_ref, sc_ref, g_ref, rows):
    y = _rms(x_ref[rows, :], g_ref[...])
    h_ref[rows, :] = (y * (1.0 + _mod_rows(sc_ref, rows)) + _mod_rows(sh_ref, rows)).astype(BF16)


def _prenorm_into(h_ref, x_ref, sh_ref, sc_ref, g_ref):
    _row_chunks(x_ref.shape[0], functools.partial(_prenorm_rows, h_ref, x_ref, sh_ref, sc_ref, g_ref))


def _postnorm_into(o_ref, y_ref, x_ref, gt_ref, g_ref, coef, clear_y=False):
    def body(rows):
        o_ref[rows, :] = x_ref[rows, :] + (coef * _mod_rows(gt_ref, rows)) * _rms(y_ref[rows, :], g_ref[...])
        if clear_y:
            y_ref[rows, :] = jnp.zeros((ROW_CHUNK, y_ref.shape[1]), y_ref.dtype)
    _row_chunks(x_ref.shape[0], body)


class Stream:
    def __init__(self, x, mod, per_row, bm=None):
        self.x, self.mod, self.per_row = x, mod, per_row
        self.G, self.R, self.D = x.shape
        self.bm = min(512, self.R) if bm is None else bm

    def with_x(self, x):
        return Stream(x, self.mod, self.per_row)

    def rows(self, width):
        return pl.BlockSpec((None, self.bm, width), lambda g, i, *_: (g, i, 0))

    def mod_spec(self, li, k):
        G, per_row = self.G, self.per_row
        rm = self.bm if per_row else 1
        return pl.BlockSpec((None, rm, self.D), lambda g, i, *_: (li * G + g, i if per_row else 0, k))


def _gain_spec(idx, width):
    return pl.BlockSpec((None, 1, width), lambda *_: (idx, 0, 0))


def _ada_kernel(c_ref, w_ref, b_ref, o_ref):
    cs = _silu(c_ref[...]).astype(BF16)
    o_ref[...] = _dot(cs, w_ref[...].astype(BF16)) + b_ref[...]


def ada_mod(c_all, ada_w, ada_b):
    Bc, D = c_all.shape
    N = ada_w.shape[-1]
    bn = 1024
    return pl.pallas_call(
        _ada_kernel,
        grid=(DEPTH, N // bn),
        in_specs=[pl.BlockSpec((Bc, D), lambda l, j: (0, 0)),
                  pl.BlockSpec((None, D, bn), lambda l, j: (l, 0, j)),
                  pl.BlockSpec((None, 1, bn), lambda l, j: (l, 0, j))],
        out_specs=pl.BlockSpec((None, Bc, bn), lambda l, j: (l, 0, j)),
        out_shape=jax.ShapeDtypeStruct((DEPTH, Bc, N), F32),
        compiler_params=_cparams("parallel", "parallel"),
        name="ada_mod",
    )(c_all, ada_w, ada_b.reshape(DEPTH, 1, N))


FFN_BF = 256
FFN_BM = 1024


def _swiglu_step(h, wg, wu, wd):
    g = _dot(h, wg)
    u = _dot(h, wu)
    return _dot((_silu(g) * u).astype(BF16), wd)


def _ffn_emit_kernel(x_ref, sh_ref, sc_ref, gt_ref, gpre_ref, gpost_ref, wg_ref, wu_ref, wd_ref,
                     o_ref, wgo_ref, wuo_ref, wdo_ref, h_ref, acc_ref):
    j = pl.program_id(0)

    @pl.when(j == 0)
    def _():
        _prenorm_into(h_ref, x_ref, sh_ref, sc_ref, gpre_ref)
        acc_ref[...] = jnp.zeros_like(acc_ref)

    wg, wu, wd = wg_ref[...].astype(BF16), wu_ref[...].astype(BF16), wd_ref[...].astype(BF16)
    wgo_ref[...] = wg
    wuo_ref[...] = wu
    wdo_ref[...] = wd
    acc_ref[...] += _swiglu_step(h_ref[...], wg, wu, wd)

    @pl.when(j == pl.num_programs(0) - 1)
    def _():
        _postnorm_into(o_ref, acc_ref, x_ref, gt_ref, gpost_ref, FFN_RES)


def ffn_emit(s, li, sub, wg, wu, wd, norm_pre, norm_post):
    assert s.G == 1 and s.R == s.bm
    D, DF = s.D, wd.shape[-2]
    k0, bf = 3 * sub, FFN_BF
    nj = DF // bf
    row = lambda width: pl.BlockSpec((None, s.bm, width), lambda j: (0, 0, 0))
    mod = lambda k: pl.BlockSpec((None, s.bm, D), lambda j: (li, 0, k))
    out = pl.pallas_call(
        _ffn_emit_kernel,
        grid=(nj,),
        in_specs=[row(D), mod(k0), mod(k0 + 1), mod(k0 + 2), _gain_spec(li * 3 + sub, D), _gain_spec(li * 3 + sub, D),
                  pl.BlockSpec((None, D, bf), lambda j: (li, 0, j)), pl.BlockSpec((None, D, bf), lambda j: (li, 0, j)),
                  pl.BlockSpec((None, bf, D), lambda j: (li, j, 0))],
        out_specs=[row(D), pl.BlockSpec((None, D, bf), lambda j: (j, 0, 0)), pl.BlockSpec((None, D, bf), lambda j: (j, 0, 0)),
                   pl.BlockSpec((bf, D), lambda j: (j, 0))],
        out_shape=[jax.ShapeDtypeStruct(s.x.shape, F32), jax.ShapeDtypeStruct((nj, D, bf), BF16),
                   jax.ShapeDtypeStruct((nj, D, bf), BF16), jax.ShapeDtypeStruct((DF, D), BF16)],
        scratch_shapes=[pltpu.VMEM((s.bm, D), BF16), pltpu.VMEM((s.bm, D), F32)],
        compiler_params=_cparams("arbitrary"),
        name="ffn_emit",
    )(s.x, s.mod, s.mod, s.mod, norm_pre, norm_post, wg, wu, wd)
    return out[0], out[1:]


def _ffn_kernel(x_ref, sh_ref, sc_ref, gt_ref, gpre_ref, gpost_ref, wg_ref, wu_ref, wd_ref, o_ref, h_ref, acc_ref):
    j = pl.program_id(2)
    first_tile = (pl.program_id(0) == 0) & (pl.program_id(1) == 0)

    @pl.when(j == 0)
    def _():
        _prenorm_into(h_ref, x_ref, sh_ref, sc_ref, gpre_ref)

        @pl.when(first_tile)
        def _():
            acc_ref[...] = jnp.zeros_like(acc_ref)

    acc_ref[...] += _swiglu_step(h_ref[...], wg_ref[...], wu_ref[...], wd_ref[...])

    @pl.when(j == pl.num_programs(2) - 1)
    def _():
        _postnorm_into(o_ref, acc_ref, x_ref, gt_ref, gpost_ref, FFN_RES, clear_y=True)


def ffn(s, li, sub, wg, wu, wd, norm_pre, norm_post):
    assert not s.per_row
    G, D = s.G, s.D
    DF = wd.shape[0]
    k0, bf = 3 * sub, FFN_BF
    bm = min(FFN_BM, s.R)
    rows = pl.BlockSpec((None, bm, D), lambda g, i, j: (g, i, 0))
    mod = lambda k: pl.BlockSpec((None, 1, D), lambda g, i, j: (li * G + g, 0, k))
    return pl.pallas_call(
        _ffn_kernel,
        grid=(G, s.R // bm, DF // bf),
        in_specs=[rows, mod(k0), mod(k0 + 1), mod(k0 + 2), _gain_spec(li * 3 + sub, D), _gain_spec(li * 3 + sub, D),
                  pl.BlockSpec((None, D, bf), lambda g, i, j: (j, 0, 0)), pl.BlockSpec((None, D, bf), lambda g, i, j: (j, 0, 0)),
                  pl.BlockSpec((bf, D), lambda g, i, j: (j, 0))],
        out_specs=rows,
        out_shape=jax.ShapeDtypeStruct(s.x.shape, F32),
        scratch_shapes=[pltpu.VMEM((bm, D), BF16), pltpu.VMEM((bm, D), F32)],
        compiler_params=_cparams("arbitrary", "arbitrary", "arbitrary"),
        name="ffn",
    )(s.x, s.mod, s.mod, s.mod, norm_pre, norm_post, wg, wu, wd)


def _mm_post_kernel(a_ref, w_ref, x_ref, gt_ref, gpost_ref, o_ref):
    y = _dot(a_ref[...], w_ref[...])
    o_ref[...] = x_ref[...] + gt_ref[...] * _rms(y, gpost_ref[...])


def mm_post(s, li, a, w, norm_post):
    K, D = w.shape
    bm = min(256, s.R)
    rows = lambda width: pl.BlockSpec((None, bm, width), lambda g, i: (g, i, 0))
    G, per_row = s.G, s.per_row
    gate = pl.BlockSpec((None, bm if per_row else 1, D), lambda g, i: (li * G + g, i if per_row else 0, 5))
    return pl.pallas_call(
        _mm_post_kernel,
        grid=(s.G, s.R // bm),
        in_specs=[rows(K), pl.BlockSpec((K, D), lambda g, i: (0, 0)), rows(D), gate, _gain_spec(li * 3 + 1, D)],
        out_specs=rows(D),
        out_shape=jax.ShapeDtypeStruct(s.x.shape, F32),
        compiler_params=_cparams("parallel", "parallel"),
        name="mm_post",
    )(a, w, s.x, s.mod, norm_post)


def _mla_proj_kernel(x_ref, sh_ref, sc_ref, gpre_ref, w_ref, gq_ref, gkv_ref, cs_ref, ql_ref, kv_ref, kvb_ref):
    h = (_rms(x_ref[...], gpre_ref[...]) * (1.0 + sc_ref[...]) + sh_ref[...]).astype(BF16)
    r = _dot(h, w_ref[...])
    ql_ref[...] = _rms(r[:, :Q_LORA], gq_ref[...]).astype(BF16)
    ckv = _rms(r[:, Q_LORA:Q_LORA + KV_LORA], gkv_ref[...])
    t = r[:, Q_LORA + KV_LORA:] * cs_ref[...]
    kr = t[:, :QK_ROPE] + t[:, QK_ROPE:]
    kv_ref[:, :KV_LORA] = ckv
    kv_ref[:, KV_LORA:] = kr
    kvb_ref[:, :KV_LORA] = ckv.astype(BF16)
    kvb_ref[:, KV_LORA:] = jnp.concatenate([kr, jnp.zeros_like(kr)], axis=-1).astype(BF16)


def mla_proj(s, li, w_cat, g_q, g_kv, cs_tab, norm_pre):
    D, N = w_cat.shape
    out = (jax.ShapeDtypeStruct((s.G, s.R, Q_LORA), BF16),
           jax.ShapeDtypeStruct((s.G, s.R, KV_ROW), F32),
           jax.ShapeDtypeStruct((s.G, s.R, KV_PAD), BF16))
    return pl.pallas_call(
        _mla_proj_kernel,
        grid=(s.G, s.R // s.bm),
        in_specs=[s.rows(D), s.mod_spec(li, 3), s.mod_spec(li, 4), _gain_spec(li * 3 + 1, D),
                  pl.BlockSpec((D, N), lambda g, i: (0, 0)),
                  pl.BlockSpec((1, Q_LORA), lambda g, i: (0, 0)),
                  pl.BlockSpec((1, KV_LORA), lambda g, i: (0, 0)),
                  pl.BlockSpec((s.bm, LANE), lambda g, i: (i, 0))],
        out_specs=(s.rows(Q_LORA), s.rows(KV_ROW), s.rows(KV_PAD)),
        out_shape=out,
        compiler_params=_cparams("parallel", "parallel"),
        name="mla_proj",
    )(s.x, s.mod, s.mod, norm_pre, w_cat, g_q, g_kv, cs_tab)


def _q_expand_kernel(ql_ref, w_ref, wr_ref, cm_ref, sm_ref, q_ref, *, hb):
    ql = ql_ref[...]
    a = _dot(ql, w_ref[...])
    b = _dot(ql, wr_ref[...])
    cm, sm = cm_ref[...], sm_ref[...]
    for h in range(hb):
        lo = h * QK_PAD
        q_ref[:, lo:lo + LANE] = (a[:, lo:lo + LANE] * cm[:, :LANE]).astype(BF16)
        hi = a[:, lo + LANE:lo + QK_PAD] * cm[:, LANE:] + b[:, h * LANE:(h + 1) * LANE] * sm[:, LANE:]
        q_ref[:, lo + LANE:lo + QK_PAD] = hi.astype(BF16)


def q_expand(s, ql, w_q, w_qrot, cmap, smap):
    hb = 4
    nw = hb * QK_PAD
    wspec = pl.BlockSpec((Q_LORA, nw), lambda g, i, j: (0, j))
    wrspec = pl.BlockSpec((Q_LORA, hb * LANE), lambda g, i, j: (0, j))
    tspec = pl.BlockSpec((s.bm, QK_PAD), lambda g, i, j: (i, 0))
    return pl.pallas_call(
        functools.partial(_q_expand_kernel, hb=hb),
        grid=(s.G, s.R // s.bm, MLA_HEADS // hb),
        in_specs=[s.rows(Q_LORA), wspec, wrspec, tspec, tspec],
        out_specs=pl.BlockSpec((None, s.bm, nw), lambda g, i, j: (g, i, j)),
        out_shape=jax.ShapeDtypeStruct((s.G, s.R, MLA_HEADS * QK_PAD), BF16),
        compiler_params=_cparams("parallel", "parallel", "parallel"),
        name="q_expand",
    )(ql, w_q, w_qrot, cmap, smap)


def _kv_expand_kernel(kvb_ref, wk_ref, wv_ref, k_ref, v_ref):
    kvb = kvb_ref[...]
    k_ref[...] = _dot(kvb, wk_ref[...]).astype(BF16)
    v_ref[...] = _dot(kvb, wv_ref[...]).astype(BF16)


def kv_expand(s, kvb, w_k, w_v):
    hb = 4
    return pl.pallas_call(
        _kv_expand_kernel,
        grid=(s.G, s.R // s.bm, MLA_HEADS // hb),
        in_specs=[s.rows(KV_PAD),
                  pl.BlockSpec((KV_PAD, hb * QK_PAD), lambda g, i, j: (0, j)),
                  pl.BlockSpec((KV_PAD, hb * V_HEAD), lambda g, i, j: (0, j))],
        out_specs=(pl.BlockSpec((None, s.bm, hb * QK_PAD), lambda g, i, j: (g, i, j)),
                   pl.BlockSpec((None, s.bm, hb * V_HEAD), lambda g, i, j: (g, i, j))),
        out_shape=(jax.ShapeDtypeStruct((s.G, s.R, MLA_HEADS * QK_PAD), BF16),
                   jax.ShapeDtypeStruct((s.G, s.R, MLA_HEADS * V_HEAD), BF16)),
        compiler_params=_cparams("parallel", "parallel", "parallel"),
        name="kv_expand",
    )(kvb, w_k, w_v)


def _flash_kernel(q_ref, k_ref, v_ref, o_ref, m_ref, acc_ref, *, blk, hb):
    i = pl.program_id(2)
    m_ref[...] = jnp.full_like(m_ref, NEG_INF)
    acc_ref[...] = jnp.zeros_like(acc_ref)
    ones_col = jnp.where(lax.broadcasted_iota(jnp.int32, (blk, QK_PAD - V_HEAD), 1) == 0, 1.0, 0.0).astype(BF16)

    def block(j, diagonal):
        r0 = pl.multiple_of(j * blk, blk)
        for h in range(hb):
            hs = slice(h * QK_PAD, (h + 1) * QK_PAD)
            v_ext = jnp.concatenate([v_ref[pl.ds(r0, blk), h * V_HEAD:(h + 1) * V_HEAD], ones_col], axis=1)
            s = _dot_nt(q_ref[:, hs], k_ref[pl.ds(r0, blk), hs])
            if diagonal:
                row = lax.broadcasted_iota(jnp.int32, (blk, blk), 0)
                col = lax.broadcasted_iota(jnp.int32, (blk, blk), 1)
                s = jnp.where(col <= row, s, NEG_INF)
            m_prev = m_ref[h]
            m_new = jnp.maximum(m_prev, jnp.max(s, axis=-1, keepdims=True))
            alpha = jnp.exp(m_prev - m_new)
            p = jnp.exp(s - jnp.concatenate([m_new] * (blk // LANE), axis=1))
            pv = _dot(p.astype(BF16), v_ext)
            acc_ref[h] = jnp.concatenate([alpha] * (QK_PAD // LANE), axis=1) * acc_ref[h] + pv
            m_ref[h] = m_new

    def below_diagonal(jj, carry):
        block(2 * jj, False)
        block(2 * jj + 1, False)
        return carry

    lax.fori_loop(0, i // 2, below_diagonal, 0)

    @pl.when(i % 2 == 1)
    def _():
        block(i - 1, False)

    block(i, True)
    for h in range(hb):
        acc = acc_ref[h]
        o_ref[:, h * V_HEAD:(h + 1) * V_HEAD] = (acc[:, :V_HEAD] / acc[:, V_HEAD:V_HEAD + 1]).astype(o_ref.dtype)


def flash_prompt(q, k, v):
    B, L, _ = q.shape
    blk = min(512, L)
    hb = 4
    seq = lambda width: pl.BlockSpec((None, L, hb * width), lambda b, h, i: (b, 0, h))
    return pl.pallas_call(
        functools.partial(_flash_kernel, blk=blk, hb=hb),
        grid=(B, MLA_HEADS // hb, L // blk),
        in_specs=[pl.BlockSpec((None, blk, hb * QK_PAD), lambda b, h, i: (b, i, h)), seq(QK_PAD), seq(V_HEAD)],
        out_specs=pl.BlockSpec((None, blk, hb * V_HEAD), lambda b, h, i: (b, i, h)),
        out_shape=jax.ShapeDtypeStruct((B, L, MLA_HEADS * V_HEAD), BF16),
        scratch_shapes=[pltpu.VMEM((hb, blk, LANE), F32), pltpu.VMEM((hb, blk, QK_PAD), F32)],
        compiler_params=_cparams("parallel", "parallel", "parallel"),
        name="flash_prompt",
    )(q, k, v)


def _q_absorb_kernel(q_ref, w_ref, qc_ref):
    q = q_ref[...]
    qc_ref[:, :KV_LORA] = _dot(q[:, :QK_NOPE], w_ref[...]).astype(BF16)
    qc_ref[:, KV_LORA:] = q[:, QK_NOPE:]


def q_absorb(q, w_ukT):
    R = q.shape[0]
    return pl.pallas_call(
        _q_absorb_kernel,
        grid=(MLA_HEADS,),
        in_specs=[pl.BlockSpec((R, QK_PAD), lambda h: (0, h)),
                  pl.BlockSpec((None, QK_NOPE, KV_LORA), lambda h: (h, 0, 0))],
        out_specs=pl.BlockSpec((None, R, KV_PAD), lambda h: (h, 0, 0)),
        out_shape=jax.ShapeDtypeStruct((MLA_HEADS, R, KV_PAD), BF16),
        compiler_params=_cparams("parallel"),
        name="q_absorb",
    )(q, w_ukT)


PAGE_SLOTS = 3


def _paged_kernel(pt_ref, qc_ref, newt_ref, cache_ref, o_ref, buf_ref, sem_ref, kb_ref, kn_ref, m_ref, l_ref, acc_ref,
                  *, layer, pp, group, steps_per_seq, heads):
    t = pl.program_id(0)
    n_steps = pl.num_programs(0)
    j = t % steps_per_seq
    rows = qc_ref.shape[0]
    gw = group * PAGE_SIZE

    def page_copy(step, k):
        slot = step % PAGE_SLOTS
        page = pt_ref[step * pp + k]
        return pltpu.make_async_copy(cache_ref.at[layer, page], buf_ref.at[slot, k], sem_ref.at[slot])

    def fetch(step):
        for k in range(pp):
            page_copy(step, k).start(priority=k % 2)

    @pl.when(t == 0)
    def _():
        for ahead in range(PAGE_SLOTS - 1):
            @pl.when(ahead < n_steps)
            def _():
                fetch(ahead)

    @pl.when(t + (PAGE_SLOTS - 1) < n_steps)
    def _():
        fetch(t + (PAGE_SLOTS - 1))

    @pl.when(j == 0)
    def _():
        m_ref[...] = jnp.full_like(m_ref, NEG_INF)
        l_ref[...] = jnp.zeros_like(l_ref)
        acc_ref[...] = jnp.zeros_like(acc_ref)
        kb_ref[KV_ROW:, :] = jnp.zeros((KV_PAD - KV_ROW, kb_ref.shape[1]), BF16)

    def update(key_refs, visible):
        q = qc_ref[...]
        ss = [_dot(q, kr[...]) for kr in key_refs]
        if visible is not None:
            ss = [jnp.where(visible, s, NEG_INF) for s in ss]
        m_prev = m_ref[...]
        m_new = m_prev
        for s in ss:
            m_new = jnp.maximum(m_new, jnp.max(s, axis=-1, keepdims=True))
        alpha = jnp.exp(m_prev - m_new)
        l_new = alpha * l_ref[...]
        acc = alpha * acc_ref[...]
        for s, kr in zip(ss, key_refs):
            p = jnp.exp(s - m_new)
            l_new = l_new + jnp.sum(p, axis=-1, keepdims=True)
            acc = acc + _dot_nt(p.astype(BF16), kr[:KV_LORA, :])
        l_ref[...] = l_new
        acc_ref[...] = acc
        m_ref[...] = m_new

    slot = t % PAGE_SLOTS
    for k in range(pp):
        page_copy(t, k).wait()
    for k in range(pp):
        kb_ref[:KV_ROW, k * PAGE_SIZE:(k + 1) * PAGE_SIZE] = buf_ref[slot, k].astype(BF16)
    update([kb_ref.at[:, g * gw:(g + 1) * gw] for g in range(pp // group)], None)

    @pl.when(j == steps_per_seq - 1)
    def _():
        T = newt_ref.shape[1]
        kn_ref[...] = jnp.zeros_like(kn_ref)
        kn_ref[:KV_ROW, :T] = newt_ref[...].astype(BF16)
        row = lax.broadcasted_iota(jnp.int32, (rows, PAGE_SIZE), 0)
        col = lax.broadcasted_iota(jnp.int32, (rows, PAGE_SIZE), 1)
        update([kn_ref], col * heads <= row)
        o_ref[...] = acc_ref[...] / l_ref[...]


def paged_attention(qc, kv_new, cache_t, layer, page_table):
    Bd, rows, _ = qc.shape
    T = kv_new.shape[1]
    n_pages = page_table.shape[1]
    pp = 32 if n_pages % 32 == 0 else n_pages
    group = 8 if pp % 8 == 0 else pp
    sps = n_pages // pp
    grid_spec = pltpu.PrefetchScalarGridSpec(
        num_scalar_prefetch=1,
        grid=(Bd * sps,),
        in_specs=[pl.BlockSpec((None, rows, KV_PAD), lambda t, pt: (t // sps, 0, 0)),
                  pl.BlockSpec((None, KV_ROW, T), lambda t, pt: (t // sps, 0, 0)),
                  pl.BlockSpec(memory_space=pl.ANY)],
        out_specs=pl.BlockSpec((None, rows, KV_LORA), lambda t, pt: (t // sps, 0, 0)),
        scratch_shapes=[pltpu.VMEM((PAGE_SLOTS, pp, KV_ROW, PAGE_SIZE), F32), pltpu.SemaphoreType.DMA((PAGE_SLOTS,)),
                        pltpu.VMEM((KV_PAD, pp * PAGE_SIZE), BF16), pltpu.VMEM((KV_PAD, PAGE_SIZE), BF16),
                        pltpu.VMEM((rows, 1), F32), pltpu.VMEM((rows, 1), F32), pltpu.VMEM((rows, KV_LORA), F32)])
    return pl.pallas_call(
        functools.partial(_paged_kernel, layer=layer, pp=pp, group=group, steps_per_seq=sps, heads=rows // T),
        grid_spec=grid_spec,
        out_shape=jax.ShapeDtypeStruct((Bd, rows, KV_LORA), F32),
        compiler_params=_cparams("arbitrary"),
        name="paged_attention",
    )(page_table.reshape(-1), qc, kv_new.transpose(0, 2, 1), cache_t)


def _v_up_kernel(o_ref, w_ref, y_ref):
    y_ref[...] = _dot(o_ref[...].astype(BF16), w_ref[...]).astype(BF16)


def v_up(o_lat, w_uv):
    R = o_lat.shape[0]
    return pl.pallas_call(
        _v_up_kernel,
        grid=(MLA_HEADS,),
        in_specs=[pl.BlockSpec((R, KV_LORA), lambda h: (0, h)), pl.BlockSpec((KV_LORA, V_HEAD), lambda h: (0, h))],
        out_specs=pl.BlockSpec((R, V_HEAD), lambda h: (0, h)),
        out_shape=jax.ShapeDtypeStruct((R, MLA_HEADS * V_HEAD), BF16),
        compiler_params=_cparams("parallel"),
        name="v_up",
    )(o_lat, w_uv)


def _hg_proj_kernel(x_ref, sh_ref, sc_ref, gpre_ref, wq_ref, wf_ref, wi_ref, wg_ref, lb_ref,
                    q_ref, k_ref, v_ref, lf_ref, g_ref, h_ref):
    @pl.when(pl.program_id(2) == 0)
    def _():
        _prenorm_into(h_ref, x_ref, sh_ref, sc_ref, gpre_ref)

    h = h_ref[...]
    q_ref[...] = _silu(_dot(h, wq_ref[...])).astype(q_ref.dtype)
    fz = _dot(h, wf_ref[...])
    lb = lb_ref[...]
    a = jnp.log(lb)
    b = jnp.log1p(-lb) + (jnp.minimum(fz, 0.0) - jnp.log1p(jnp.exp(-jnp.abs(fz))))
    lf_ref[...] = jnp.maximum(a, b) + jnp.log1p(jnp.exp(-jnp.abs(a - b)))
    k_ref[...] = ((1.0 - lb) * _sigmoid(-fz)).astype(k_ref.dtype)
    v_ref[...] = _dot(h, wi_ref[...]).astype(v_ref.dtype)
    g_ref[...] = _silu(_dot(h, wg_ref[...])).astype(g_ref.dtype)


PROJ_BM = 1024
PROJ_BN = 256


def _blocked(w):
    D, N = w.shape
    return w.reshape(D, N // PROJ_BN, PROJ_BN).transpose(1, 0, 2).astype(BF16)


def hg_proj(s, li, w_q, w_f, w_i, w_g, lb, norm_pre):
    s = Stream(s.x, s.mod, s.per_row, bm=min(PROJ_BM, s.R))
    nb, D, bn = w_q.shape
    wspec = pl.BlockSpec((None, D, bn), lambda g, i, j: (j, 0, 0))
    ospec = pl.BlockSpec((None, s.bm, bn), lambda g, i, j: (g, i, j))
    shape = lambda dt: jax.ShapeDtypeStruct((s.G, s.R, nb * bn), dt)
    return pl.pallas_call(
        _hg_proj_kernel,
        grid=(s.G, s.R // s.bm, nb),
        in_specs=[s.rows(D), s.mod_spec(li, 3), s.mod_spec(li, 4), _gain_spec(li * 3 + 1, D),
                  wspec, wspec, wspec, wspec, pl.BlockSpec((1, bn), lambda g, i, j: (0, j))],
        out_specs=(ospec,) * 5,
        out_shape=(shape(BF16), shape(BF16), shape(BF16), shape(F32), shape(BF16)),
        scratch_shapes=[pltpu.VMEM((s.bm, D), BF16)],
        compiler_params=_cparams("parallel", "parallel", "arbitrary"),
        name="hg_proj",
    )(s.x, s.mod, s.mod, norm_pre, w_q, w_f, w_i, w_g, lb)


def _cumsum_rows(x, span):
    row = lax.broadcasted_iota(jnp.int32, x.shape, 0)
    pos = jnp.bitwise_and(row, span - 1)
    sh = 1
    while sh < span:
        x = x + jnp.where(pos >= sh, pltpu.roll(x, sh, 0), 0.0)
        sh *= 2
    return x


def _chunk_id(r):
    return lax.shift_right_logical(r, HG_CHUNK.bit_length() - 1)


def _hg_scan_kernel(q_ref, k_ref, v_ref, lf_ref, g_ref, s0_ref, gn_ref, o_ref, so_ref, st_ref, *, hb, chained):
    n = HG_BLOCK // HG_CHUNK
    first = pl.program_id(2) == 0
    last = pl.program_id(2) == pl.num_programs(2) - 1

    if chained:
        @pl.when(first)
        def _():
            for h in range(hb):
                st_ref[h] = s0_ref[h].T

    row = lax.broadcasted_iota(jnp.int32, (HG_BLOCK, HG_BLOCK), 0)
    col = lax.broadcasted_iota(jnp.int32, (HG_BLOCK, HG_BLOCK), 1)
    causal = (_chunk_id(row) == _chunk_id(col)) & (col <= row)
    chunk_of_row = _chunk_id(lax.broadcasted_iota(jnp.int32, (HG_BLOCK, HG_DK), 0))

    for h in range(hb):
        sl = slice(h * HG_DK, (h + 1) * HG_DK)
        b = _cumsum_rows(lf_ref[:, sl], HG_CHUNK)
        b_last = jnp.concatenate(
            [jnp.broadcast_to(b[c * HG_CHUNK + HG_CHUNK - 1:(c + 1) * HG_CHUNK, :], (HG_CHUNK, HG_DK)) for c in range(n)],
            axis=0)
        q, k, v = q_ref[:, sl].astype(F32), k_ref[:, sl].astype(F32), v_ref[:, sl].astype(F32)
        qg = q * jnp.exp(b)
        kd = (k * jnp.exp(-b)).astype(BF16)
        kl = k * jnp.exp(b_last - b)
        qgb = qg.astype(BF16)
        vb = v.astype(BF16)
        a = jnp.where(causal, _dot_nt(qgb, kd), 0.0)
        o = _dot(a.astype(BF16), vb)
        klx = jnp.concatenate([jnp.where(chunk_of_row == c, kl, 0.0).astype(BF16) for c in range(n)], axis=1)
        qx = jnp.concatenate([jnp.where(chunk_of_row == c, qg, 0.0).astype(BF16) for c in range(n)], axis=1)
        if chained:
            ut = _dot(v.T.astype(BF16), klx)
            sts = []
            st = st_ref[h]
            for c in range(n):
                sts.append(st.astype(BF16))
                dec = jnp.exp(b_last[c * HG_CHUNK:c * HG_CHUNK + 1, :])
                st = st * dec + ut[:, c * HG_DK:(c + 1) * HG_DK]
            st_ref[h] = st

            @pl.when(last)
            def _():
                so_ref[h] = st.T
            o = o + _dot_nt(qx, jnp.concatenate(sts, axis=1))
        else:
            s0 = [s0_ref[c, h] for c in range(n)]
            dec_rows = jnp.concatenate([b_last[c * HG_CHUNK:c * HG_CHUNK + 1, :] for c in range(n)], axis=0)
            dec_cols = jnp.exp(dec_rows).T
            u_all = lax.dot_general(klx, vb, (((0,), (0,)), ((), ())), preferred_element_type=F32)
            for c in range(n):
                so_ref[c, h] = s0[c] * dec_cols[:, c:c + 1] + u_all[c * HG_DK:(c + 1) * HG_DK, :]
            o = o + _dot(qx, jnp.concatenate([s.astype(BF16) for s in s0], axis=0))
        o_ref[:, sl] = (_rms(o, gn_ref[...]) * g_ref[:, sl]).astype(BF16)


def hg_scan(q, k, v, lf, gate, s0, g_norm, chained):
    G, R, N = q.shape
    hb = 8 if chained else 4
    n = HG_BLOCK // HG_CHUNK
    rows = pl.BlockSpec((None, HG_BLOCK, hb * HG_DK), lambda g, h, l: (g, l, h))
    if chained:
        sspec = pl.BlockSpec((None, hb, HG_DK, HG_DV), lambda g, h, l: (g, h, 0, 0))
    else:
        sspec = pl.BlockSpec((None, n, hb, HG_DK, HG_DV), lambda g, h, l: (g, 0, h, 0, 0))
    return pl.pallas_call(
        functools.partial(_hg_scan_kernel, hb=hb, chained=chained),
        grid=(G, HG_HEADS // hb, R // HG_BLOCK),
        in_specs=[rows, rows, rows, rows, rows, sspec, pl.BlockSpec((1, HG_DV), lambda g, h, l: (0, 0))],
        out_specs=(rows, sspec),
        out_shape=(jax.ShapeDtypeStruct((G, R, N), BF16), jax.ShapeDtypeStruct(s0.shape, F32)),
        scratch_shapes=[pltpu.VMEM((hb, HG_DV, HG_DK), F32)],
        compiler_params=_cparams("parallel", "parallel", "arbitrary"),
        name="hg_scan",
    )(q, k, v, lf, gate, s0, g_norm)


def _lru_proj_kernel(x_ref, sh_ref, sc_ref, gpre_ref, wx_ref, wy_ref, u_ref, g_ref, h_ref):
    @pl.when(pl.program_id(2) == 0)
    def _():
        _prenorm_into(h_ref, x_ref, sh_ref, sc_ref, gpre_ref)

    h = h_ref[...]
    u_ref[...] = _dot(h, wx_ref[...])
    g_ref[...] = _gelu_tanh(_dot(h, wy_ref[...])).astype(g_ref.dtype)


def lru_proj(s, li, w_x, w_y, norm_pre):
    s = Stream(s.x, s.mod, s.per_row, bm=min(PROJ_BM, s.R))
    nb, D, bn = w_x.shape
    wspec = pl.BlockSpec((None, D, bn), lambda g, i, j: (j, 0, 0))
    ospec = pl.BlockSpec((None, s.bm, bn), lambda g, i, j: (g, i, j))
    return pl.pallas_call(
        _lru_proj_kernel,
        grid=(s.G, s.R // s.bm, nb),
        in_specs=[s.rows(D), s.mod_spec(li, 3), s.mod_spec(li, 4), _gain_spec(li * 3 + 1, D), wspec, wspec],
        out_specs=(ospec, ospec),
        out_shape=(jax.ShapeDtypeStruct((s.G, s.R, nb * bn), F32), jax.ShapeDtypeStruct((s.G, s.R, nb * bn), BF16)),
        scratch_shapes=[pltpu.VMEM((s.bm, D), BF16)],
        compiler_params=_cparams("parallel", "parallel", "arbitrary"),
        name="lru_proj",
    )(s.x, s.mod, s.mod, norm_pre, w_x, w_y)


def _lru_gates(xc, n, wra_ref, bra_ref, wix_ref, bix_ref, lam_ref):
    sl = slice(n * LRU_BW, (n + 1) * LRU_BW)
    xb = xc.astype(BF16)
    r = _sigmoid(_dot(xb, wra_ref[n]) + bra_ref[:, sl])
    ig = _sigmoid(_dot(xb, wix_ref[n]) + bix_ref[:, sl])
    log_a = -LRU_C * r * _softplus(-lam_ref[:, sl])
    a = jnp.exp(log_a)
    return a, jnp.sqrt(-jnp.tanh(log_a) * (a * a + 1.0)) * (ig * xc)


def _lru_scan_kernel(u_ref, gate_ref, cb_ref, h0_ref, cw_ref, cbias_ref, wra_ref, bra_ref, wix_ref, bix_ref, lam_ref,
                     yg_ref, hl_ref, cbo_ref, ext_ref, a_ref, b_ref, hc_ref, *, bl):
    l = pl.program_id(1)
    keep = CONV_W - 1
    base = 8

    @pl.when(l == 0)
    def _():
        ext_ref[base - keep:base, :] = cb_ref[...]
        hc_ref[...] = h0_ref[...]

    ext_ref[base:base + bl, :] = u_ref[...]
    for n in range(LRU_BLOCKS):
        sl = slice(n * LRU_BW, (n + 1) * LRU_BW)
        xc = cbias_ref[:, sl]
        for j in range(CONV_W):
            xc = xc + ext_ref[base - keep + j:base - keep + j + bl, sl] * cw_ref[j:j + 1, sl]
        a, b = _lru_gates(xc, n, wra_ref, bra_ref, wix_ref, bix_ref, lam_ref)
        a_ref[:, sl] = a
        b_ref[:, sl] = b

    pos = lax.broadcasted_iota(jnp.int32, (8, a_ref.shape[1]), 0)

    def tile(t, h_prev):
        r0 = pl.multiple_of(t * 8, 8)
        a_t = a_ref[pl.ds(r0, 8), :]
        b_t = b_ref[pl.ds(r0, 8), :]
        for sh in (1, 2, 4):
            m = pos >= sh
            b_t = jnp.where(m, a_t * pltpu.roll(b_t, sh, 0) + b_t, b_t)
            a_t = jnp.where(m, a_t * pltpu.roll(a_t, sh, 0), a_t)
        h_t = b_t + a_t * h_prev
        b_ref[pl.ds(r0, 8), :] = h_t
        return h_t[7:8, :]

    h_last = lax.fori_loop(0, bl // 8, tile, hc_ref[...])
    hc_ref[...] = h_last
    yg_ref[...] = (b_ref[...] * gate_ref[...]).astype(BF16)
    tail = ext_ref[base + bl - keep:base + bl, :]
    ext_ref[base - keep:base, :] = tail

    @pl.when(l == pl.num_programs(1) - 1)
    def _():
        hl_ref[...] = h_last
        cbo_ref[...] = tail


def lru_scan(u, gate, conv_buf, h0, conv_w, conv_b, w_ra, b_ra, w_ix, b_ix, lam):
    B, L, W = u.shape
    bl = min(256, L)
    rows = pl.BlockSpec((None, bl, W), lambda b, l: (b, l, 0))
    vec = pl.BlockSpec((1, W), lambda b, l: (0, 0))
    wblk = pl.BlockSpec((LRU_BLOCKS, LRU_BW, LRU_BW), lambda b, l: (0, 0, 0))
    hspec = pl.BlockSpec((None, 1, W), lambda b, l: (b, 0, 0))
    cspec = pl.BlockSpec((None, CONV_W - 1, W), lambda b, l: (b, 0, 0))
    return pl.pallas_call(
        functools.partial(_lru_scan_kernel, bl=bl),
        grid=(B, L // bl),
        in_specs=[rows, rows, cspec, hspec, pl.BlockSpec((CONV_W, W), lambda b, l: (0, 0)), vec,
                  wblk, vec, wblk, vec, vec],
        out_specs=(rows, hspec, cspec),
        out_shape=(jax.ShapeDtypeStruct((B, L, W), BF16), jax.ShapeDtypeStruct((B, 1, W), F32),
                   jax.ShapeDtypeStruct((B, CONV_W - 1, W), F32)),
        scratch_shapes=[pltpu.VMEM((bl + 8, W), F32), pltpu.VMEM((bl, W), F32), pltpu.VMEM((bl, W), F32),
                        pltpu.VMEM((1, W), F32)],
        compiler_params=_cparams("parallel", "arbitrary"),
        name="lru_scan",
    )(u, gate, conv_buf, h0.reshape(B, 1, W), conv_w, conv_b, w_ra, b_ra, w_ix, b_ix, lam)


def _lru_step_kernel(u_ref, gate_ref, cb_ref, h0_ref, cw_ref, cbias_ref, wra_ref, bra_ref, wix_ref, bix_ref, lam_ref,
                     yg_ref, hl_ref, cbo_ref):
    T = u_ref.shape[0]
    keep = CONV_W - 1
    for n in range(LRU_BLOCKS):
        sl = slice(n * LRU_BW, (n + 1) * LRU_BW)
        ext = [cb_ref[j, :, sl] for j in range(keep)] + [u_ref[t, :, sl] for t in range(T)]
        h = h0_ref[:, sl]
        for t in range(T):
            xc = cbias_ref[:, sl]
            for j in range(CONV_W):
                xc = xc + ext[t + j] * cw_ref[j:j + 1, sl]
            a, b = _lru_gates(xc, n, wra_ref, bra_ref, wix_ref, bix_ref, lam_ref)
            h = a * h + b
            yg_ref[t, :, sl] = (h * gate_ref[t, :, sl]).astype(BF16)
        hl_ref[:, sl] = h
        for j in range(keep):
            cbo_ref[j, :, sl] = ext[T + j]


def lru_step(u, gate, conv_buf, h0, conv_w, conv_b, w_ra, b_ra, w_ix, b_ix, lam):
    T, Bd, W = u.shape
    full = lambda *shape: pl.BlockSpec(shape, lambda i: (0,) * len(shape))
    return pl.pallas_call(
        _lru_step_kernel,
        grid=(1,),
        in_specs=[full(T, Bd, W), full(T, Bd, W), full(CONV_W - 1, Bd, W), full(Bd, W), full(CONV_W, W), full(1, W),
                  full(LRU_BLOCKS, LRU_BW, LRU_BW), full(1, W), full(LRU_BLOCKS, LRU_BW, LRU_BW), full(1, W), full(1, W)],
        out_specs=(full(T, Bd, W), full(Bd, W), full(CONV_W - 1, Bd, W)),
        out_shape=(jax.ShapeDtypeStruct((T, Bd, W), BF16), jax.ShapeDtypeStruct((Bd, W), F32),
                   jax.ShapeDtypeStruct((CONV_W - 1, Bd, W), F32)),
        compiler_params=_cparams("arbitrary"),
        name="lru_step",
    )(u, gate, conv_buf, h0, conv_w, conv_b, w_ra, b_ra, w_ix, b_ix, lam)


def _rope_tables(pos):
    half = QK_ROPE // 2
    inv = ROPE_THETA ** (-jnp.arange(half, dtype=F32) / half)
    ang = pos.astype(F32)[:, None] * inv[None, :]
    cos, sin = jnp.cos(ang), jnp.sin(ang)
    cos2, sin2 = jnp.concatenate([cos, cos], -1), jnp.concatenate([sin, sin], -1)
    ones, zeros = jnp.ones((pos.shape[0], QK_NOPE), F32), jnp.zeros((pos.shape[0], QK_NOPE), F32)
    pad = jnp.zeros((pos.shape[0], QK_PAD - QK_NOPE - QK_ROPE), F32)
    cs_tab = jnp.concatenate([cos2, sin2], -1)
    cmap = jnp.concatenate([ones, cos2, pad], -1) * MLA_SCALE
    smap = jnp.concatenate([zeros, sin2, pad], -1) * MLA_SCALE
    return cs_tab, cmap, smap


def _rot_cols(w):
    half = w.shape[-1] // 2
    return jnp.concatenate([-w[..., half:], w[..., :half]], -1)


def _mla_weights(w_dq, w_uq, w_dkv, w_uk, w_uv):
    kr = w_dkv[:, KV_LORA:]
    w_cat = jnp.concatenate([w_dq, w_dkv, _rot_cols(kr)], axis=1).astype(BF16)
    uq = w_uq.reshape(Q_LORA, MLA_HEADS, QK_NOPE + QK_ROPE)
    nope, rope = uq[..., :QK_NOPE], uq[..., QK_NOPE:]
    pad = jnp.zeros((Q_LORA, MLA_HEADS, QK_PAD - QK_NOPE - QK_ROPE), F32)
    w_q = jnp.concatenate([nope, rope, pad], -1).reshape(Q_LORA, MLA_HEADS * QK_PAD).astype(BF16)
    w_qrot = jnp.concatenate([_rot_cols(rope), pad], -1).reshape(Q_LORA, MLA_HEADS * LANE).astype(BF16)
    w_k = jnp.zeros((KV_PAD, MLA_HEADS, QK_PAD), F32)
    w_k = w_k.at[:KV_LORA, :, :QK_NOPE].set(w_uk)
    eye = jnp.broadcast_to(jnp.eye(QK_ROPE, dtype=F32)[:, None, :], (QK_ROPE, MLA_HEADS, QK_ROPE))
    w_k = w_k.at[KV_LORA:KV_ROW, :, QK_NOPE:QK_NOPE + QK_ROPE].set(eye)
    w_k = w_k.reshape(KV_PAD, MLA_HEADS * QK_PAD).astype(BF16)
    w_v = jnp.zeros((KV_PAD, MLA_HEADS * V_HEAD), F32).at[:KV_LORA].set(w_uv.reshape(KV_LORA, -1)).astype(BF16)
    w_ukT = w_uk.transpose(1, 2, 0).astype(BF16)
    w_uv2 = w_uv.reshape(KV_LORA, MLA_HEADS * V_HEAD).astype(BF16)
    return w_cat, w_q, w_qrot, w_k, w_v, w_ukT, w_uv2


def _hg_lower_bounds(lb_param):
    cs = jnp.cumsum(jax.nn.softmax(lb_param.astype(F32), axis=0), axis=0)
    return cs - cs[0]


def kernel(x_prompt, x_sample, c_prompt, c_sample, cache_mla_kv, page_table, state_hgrn, state_lru_h, state_lru_conv, ada_w, ada_b, norm_pre, norm_post, ffn1_wg, ffn1_wu, ffn1_wd, ffn2_wg, ffn2_wu, ffn2_wd, mla_w_dq, mla_g_q, mla_w_uq, mla_w_dkv, mla_g_kv, mla_w_uk, mla_w_uv, mla_w_o, hg_lb, hg_w_q, hg_w_f, hg_w_i, hg_w_g, hg_g_norm, hg_w_o, lru_w_x, lru_w_y, lru_conv_w, lru_conv_b, lru_w_ra, lru_b_ra, lru_w_ix, lru_b_ix, lru_lam, lru_w_o):
    B, L, D = x_prompt.shape
    Bd, T, _ = x_sample.shape
    n_pages = page_table.shape[1]
    past_len = n_pages * PAGE_SIZE

    n_s = Bd * T
    c_all = jnp.concatenate([jnp.repeat(c_sample, T, axis=0), c_prompt, jnp.zeros((-(n_s + B) % 8, D), F32)], axis=0)
    mod = ada_mod(c_all, ada_w, ada_b)
    mod_p = mod[:, n_s:n_s + B].reshape(DEPTH * B, 1, N_MOD * D)
    sp = Stream(x_prompt, mod_p, per_row=False)
    ss = Stream(x_sample.reshape(1, n_s, D), mod, per_row=True)

    npre = norm_pre.reshape(DEPTH * 3, 1, D)
    npost = norm_post.reshape(DEPTH * 3, 1, D)
    bf = lambda w: w.astype(BF16)
    f1 = (ffn1_wg, ffn1_wu, ffn1_wd)
    f2 = (ffn2_wg, ffn2_wu, ffn2_wd)
    cache_t = jnp.swapaxes(cache_mla_kv, 2, 3)

    tab_p = _rope_tables(jnp.arange(L))
    tab_s = _rope_tables(jnp.tile(past_len + jnp.arange(T), Bd))
    lbs = _hg_lower_bounds(hg_lb)

    kv_p, kv_s, hg_p, hg_s, lh_p, lh_s, lc_p, lc_s = [], [], [], [], [], [], [], []
    for li in range(DEPTH):
        xs, w_bf = ffn_emit(ss, li, 0, *f1, npre, npost)
        ss = ss.with_x(xs)
        sp = sp.with_x(ffn(sp, li, 0, *w_bf, npre, npost))
        kind, j = li % N_MIXERS, li // N_MIXERS
        if kind == 0:
            w_cat, w_q, w_qrot, w_k, w_v, w_ukT, w_uv2 = _mla_weights(
                mla_w_dq[j], mla_w_uq[j], mla_w_dkv[j], mla_w_uk[j], mla_w_uv[j])
            g_q, g_kv = mla_g_q[j].reshape(1, Q_LORA), mla_g_kv[j].reshape(1, KV_LORA)
            w_o = bf(mla_w_o[j])
            ql, kvr, kvb = mla_proj(sp, li, w_cat, g_q, g_kv, tab_p[0], npre)
            q = q_expand(sp, ql, w_q, w_qrot, tab_p[1], tab_p[2])
            kk, vv = kv_expand(sp, kvb, w_k, w_v)
            o = flash_prompt(q, kk, vv)
            sp = sp.with_x(mm_post(sp, li, o, w_o, npost))
            kv_p.append(kvr)
            ql, kvr, _ = mla_proj(ss, li, w_cat, g_q, g_kv, tab_s[0], npre)
            q = q_expand(ss, ql, w_q, w_qrot, tab_s[1], tab_s[2])
            qc = q_absorb(q[0], w_ukT)
            qc = qc.reshape(MLA_HEADS, Bd, T, KV_PAD).transpose(1, 2, 0, 3).reshape(Bd, T * MLA_HEADS, KV_PAD)
            kv_new = kvr.reshape(Bd, T, KV_ROW)
            o_lat = paged_attention(qc, kv_new, cache_t, j, page_table)
            o = v_up(o_lat.reshape(Bd * T, MLA_HEADS * KV_LORA), w_uv2)
            ss = ss.with_x(mm_post(ss, li, o.reshape(1, Bd * T, -1), w_o, npost))
            kv_s.append(kv_new)
        elif kind == 1:
            lb = lbs[li].reshape(1, -1)
            ws = (_blocked(hg_w_q[j]), _blocked(hg_w_f[j]), _blocked(hg_w_i[j]), _blocked(hg_w_g[j]))
            g_n = hg_g_norm[j].reshape(1, HG_DV)
            w_o = bf(hg_w_o[j])
            q, k, v, lf, gate = hg_proj(sp, li, *ws, lb, npre)
            s0 = jnp.zeros((B, HG_HEADS, HG_DK, HG_DV), F32)
            o, s_new = hg_scan(q, k, v, lf, gate, s0, g_n, chained=True)
            sp = sp.with_x(mm_post(sp, li, o, w_o, npost))
            hg_p.append(s_new)
            n_seq = HG_BLOCK // HG_CHUNK
            def to_chunks(a):
                a = jnp.pad(a.reshape(Bd, T, -1), ((0, 0), (HG_CHUNK - T, 0), (0, 0)))
                return a.reshape(Bd // n_seq, HG_BLOCK, -1)
            qs, ks, vs, lfs, gs = [to_chunks(a) for a in hg_proj(ss, li, *ws, lb, npre)]
            s0 = state_hgrn[j].reshape(Bd // n_seq, n_seq, HG_HEADS, HG_DK, HG_DV)
            o, s_new = hg_scan(qs, ks, vs, lfs, gs, s0, g_n, chained=False)
            o = o.reshape(Bd, HG_CHUNK, -1)[:, HG_CHUNK - T:].reshape(1, Bd * T, -1)
            ss = ss.with_x(mm_post(ss, li, o, w_o, npost))
            hg_s.append(s_new.reshape(Bd, HG_HEADS, HG_DK, HG_DV))
        else:
            w_x, w_y, w_o = _blocked(lru_w_x[j]), _blocked(lru_w_y[j]), bf(lru_w_o[j])
            W = w_o.shape[0]
            prm = (lru_conv_w[j], lru_conv_b[j].reshape(1, W), bf(lru_w_ra[j]), lru_b_ra[j].reshape(1, W),
                   bf(lru_w_ix[j]), lru_b_ix[j].reshape(1, W), lru_lam[j].reshape(1, W))
            u, gate = lru_proj(sp, li, w_x, w_y, npre)
            yg, h_last, cb = lru_scan(u, gate, jnp.zeros((B, CONV_W - 1, W), F32), jnp.zeros((B, W), F32), *prm)
            sp = sp.with_x(mm_post(sp, li, yg, w_o, npost))
            lh_p.append(h_last.reshape(B, W))
            lc_p.append(cb)
            u, gate = lru_proj(ss, li, w_x, w_y, npre)
            tm = lambda a: a.reshape(Bd, T, W).transpose(1, 0, 2)
            yg, h_last, cb = lru_step(tm(u), tm(gate), state_lru_conv[j].transpose(1, 0, 2), state_lru_h[j], *prm)
            ss = ss.with_x(mm_post(ss, li, yg.transpose(1, 0, 2).reshape(1, Bd * T, W), w_o, npost))
            lh_s.append(h_last)
            lc_s.append(cb.transpose(1, 0, 2))
        xs, w_bf = ffn_emit(ss, li, 2, *f2, npre, npost)
        ss = ss.with_x(xs)
        sp = sp.with_x(ffn(sp, li, 2, *w_bf, npre, npost))

    return (sp.x, ss.x.reshape(Bd, T, D), jnp.stack(kv_p), jnp.stack(kv_s), jnp.stack(hg_p), jnp.stack(hg_s),
            jnp.stack(lh_p), jnp.stack(lh_s), jnp.stack(lc_p), jnp.stack(lc_s))
```
